```python
import jax, jax.numpy as jnp
from jax import lax
import numpy as np

D_MODEL = 1024
BATCH = 16
SEQ = 2048
DEPTH = 4
DEC_BATCH = 32
DEC_SEQ = 32
PAST_LEN = 4096

CHUNK = 64
N_EVEN = (DEPTH + 1) // 2
N_ODD = DEPTH // 2
D_A = D_MODEL // 2
D_B = D_MODEL // 2
CONV_A_WIDTH = 31
POOL_WINDOWS = (2, 4, 8, 16)
N_POOL_GROUPS = len(POOL_WINDOWS)
D_POOL_GROUP = D_B // N_POOL_GROUPS
POOL_BUF = max(POOL_WINDOWS) - 1
D_IN_EVEN = 2 * D_A + D_B
D_C = D_MODEL
CONV_C_WIDTH = 3
PEER_HEADS = 8
PEER_NKEYS = 128
PEER_EXPERTS = PEER_NKEYS * PEER_NKEYS
PEER_TOPK = 16
PEER_DQ = 256
PEER_DHALF = PEER_DQ // 2
PEER_BLOCK = 128
N_MOD = 6
EPS = 1e-6

kernel_name = 'hybrid_conv_pool_peer_stream_step'


def rmsnorm(x, g):
    xf = x.astype(jnp.float32)
    y = xf * lax.rsqrt(jnp.mean(xf * xf, axis=-1, keepdims=True) + EPS)
    return (y * g.astype(jnp.float32)).astype(x.dtype)


def layernorm(x, g, b):
    xf = x.astype(jnp.float32)
    mu = jnp.mean(xf, axis=-1, keepdims=True)
    var = jnp.mean(jnp.square(xf - mu), axis=-1, keepdims=True)
    y = (xf - mu) * lax.rsqrt(var + EPS)
    return (y * g.astype(jnp.float32) + b.astype(jnp.float32)).astype(x.dtype)


def depthwise_causal_conv(ext, w):
    c = ext.shape[-1]
    return lax.conv_general_dilated(ext, w[:, None, :].astype(ext.dtype), window_strides=(1,),
                                    padding='VALID', dimension_numbers=('NWC', 'WIO', 'NWC'),
                                    feature_group_count=c)


def multiscale_pool(ext, pos0, t):
    p = POOL_BUF
    xf = ext.astype(jnp.float32)
    cs = jnp.concatenate([jnp.zeros_like(xf[:, :1]), jnp.cumsum(xf, axis=1)], axis=1)
    pos = pos0 + jnp.arange(t)
    outs = []
    for gi, w in enumerate(POOL_WINDOWS):
        sl = slice(gi * D_POOL_GROUP, (gi + 1) * D_POOL_GROUP)
        s = cs[:, p + 1:p + 1 + t, sl] - cs[:, p + 1 - w:p + 1 - w + t, sl]
        cnt = jnp.minimum(w, pos + 1).astype(jnp.float32)
        outs.append(s / cnt[None, :, None])
    return jnp.concatenate(outs, axis=-1).astype(ext.dtype)


def mix_even(h, prev_a, prev_b, pos0, w_in, dw_a, db_a, ln_g, ln_b, w_pool, pool_scale, w_out):
    bsz, t, _ = h.shape
    p = h @ w_in
    a_lin, a_gate, b_in = p[..., :D_A], p[..., D_A:2 * D_A], p[..., 2 * D_A:]
    a = a_lin * jax.nn.sigmoid(a_gate)
    ext_a = jnp.concatenate([prev_a, a], axis=1)
    a_conv = depthwise_causal_conv(ext_a, dw_a) + db_a
    a_out = jax.nn.silu(layernorm(a_conv, ln_g, ln_b))
    ext_b = jnp.concatenate([prev_b, b_in], axis=1)
    pooled = (multiscale_pool(ext_b, pos0, t) - b_in).reshape(bsz, t, N_POOL_GROUPS, D_POOL_GROUP)
    b_out = jnp.einsum('btgc,gcd->btgd', pooled, w_pool).reshape(bsz, t, D_B) * pool_scale
    y = jnp.concatenate([a_out, b_out], axis=-1) @ w_out
    return y, ext_a[:, -(CONV_A_WIDTH - 1):], ext_b[:, -POOL_BUF:]


def mix_odd(h, prev_c, w_in, dw_c, w_out):
    p = h @ w_in
    bg, cg, xt = p[..., :D_C], p[..., D_C:2 * D_C], p[..., 2 * D_C:]
    ext = jnp.concatenate([prev_c, cg * xt], axis=1)
    y = (bg * depthwise_causal_conv(ext, dw_c)) @ w_out
    return y, ext[:, -(CONV_C_WIDTH - 1):]


def peer(h, wq, k1, k2, u, v):
    bsz, t, d = h.shape
    m = bsz * t
    pad = (-m) % PEER_BLOCK
    blocks = jnp.pad(h.reshape(m, d), ((0, pad), (0, 0))).reshape(-1, PEER_BLOCK, d)

    def block_fn(hb):
        q = (hb @ wq).reshape(PEER_BLOCK, PEER_HEADS, PEER_DQ)
        s1 = jnp.einsum('phd,hkd->phk', q[..., :PEER_DHALF], k1)
        s2 = jnp.einsum('phd,hkd->phk', q[..., PEER_DHALF:], k2)
        v1, i1 = lax.top_k(s1, PEER_TOPK)
        v2, i2 = lax.top_k(s2, PEER_TOPK)
        cand = (v1[..., :, None] + v2[..., None, :]).reshape(PEER_BLOCK, PEER_HEADS, PEER_TOPK * PEER_TOPK)
        cidx = (i1[..., :, None] * PEER_NKEYS + i2[..., None, :]).reshape(PEER_BLOCK, PEER_HEADS, PEER_TOPK * PEER_TOPK)
        sc, sel = lax.top_k(cand, PEER_TOPK)
        eidx = jnp.take_along_axis(cidx, sel, axis=-1)
        g = jax.nn.softmax(sc.astype(jnp.float32), axis=-1).astype(hb.dtype)
        act = jax.nn.gelu(jnp.einsum('phkd,pd->phk', u[eidx], hb))
        return jnp.einsum('phk,phkd->pd', g * act, v[eidx])

    out = lax.map(block_fn, blocks).reshape(-1, d)[:m]
    return out.reshape(bsz, t, d)


def run_trunk(x, c, prev_a, prev_b, prev_c, pos0, norm_mix_g, norm_ffn_g, w_ada, b_ada,
              w_in_even, dw_a, db_a, ln_a_g, ln_a_b, w_pool, pool_scale, w_out_even,
              w_in_odd, dw_c, w_out_odd, peer_wq, peer_k1, peer_k2, peer_u, peer_v, final_g):
    new_a, new_b, new_c = [], [], []
    c_act = jax.nn.silu(c)
    for l in range(DEPTH):
        mod = c_act @ w_ada[l] + b_ada[l]
        sh1, sc1, g1, sh2, sc2, g2 = [mm[:, None, :] for mm in jnp.split(mod, N_MOD, axis=-1)]
        h = rmsnorm(x, norm_mix_g[l]) * (1 + sc1) + sh1
        i = l // 2
        if l % 2 == 0:
            y, sa, sb = mix_even(h, prev_a[i], prev_b[i], pos0, w_in_even[i], dw_a[i], db_a[i],
                                 ln_a_g[i], ln_a_b[i], w_pool[i], pool_scale[i], w_out_even[i])
            new_a.append(sa)
            new_b.append(sb)
        else:
            y, scc = mix_odd(h, prev_c[i], w_in_odd[i], dw_c[i], w_out_odd[i])
            new_c.append(scc)
        x = x + g1 * y
        h = rmsnorm(x, norm_ffn_g[l]) * (1 + sc2) + sh2
        x = x + g2 * peer(h, peer_wq[l], peer_k1[l], peer_k2[l], peer_u[l], peer_v[l])
    return rmsnorm(x, final_g), jnp.stack(new_a), jnp.stack(new_b), jnp.stack(new_c)


def setup_inputs(seed: int = 0) -> dict:
    key = jax.random.key(seed)
    ks = jax.random.split(key, 32)
    f32 = jnp.float32

    def nrm(k, shape, scale):
        return jax.random.normal(k, shape, f32) * scale

    return {
        'x_prompt': nrm(ks[0], (BATCH, SEQ, D_MODEL), 1.0),
        'x_sample': nrm(ks[1], (DEC_BATCH, DEC_SEQ, D_MODEL), 1.0),
        'state_conv_a': nrm(ks[2], (N_EVEN, DEC_BATCH, CONV_A_WIDTH - 1, D_A), 0.5),
        'state_pool_b': nrm(ks[3], (N_EVEN, DEC_BATCH, POOL_BUF, D_B), 0.5),
        'state_conv_c': nrm(ks[4], (N_ODD, DEC_BATCH, CONV_C_WIDTH - 1, D_C), 0.5),
        'c_prompt': nrm(ks[5], (BATCH, D_MODEL), 1.0),
        'c_sample': nrm(ks[6], (DEC_BATCH, D_MODEL), 1.0),
        'norm_mix_g': 1.0 + nrm(ks[7], (DEPTH, D_MODEL), 0.05),
        'norm_ffn_g': 1.0 + nrm(ks[8], (DEPTH, D_MODEL), 0.05),
        'w_ada': nrm(ks[9], (DEPTH, D_MODEL, N_MOD * D_MODEL), 0.5 * D_MODEL ** -0.5),
        'b_ada': nrm(ks[10], (DEPTH, N_MOD * D_MODEL), 0.02),
        'w_in_even': nrm(ks[11], (N_EVEN, D_MODEL, D_IN_EVEN), D_MODEL ** -0.5),
        'dw_a': nrm(ks[12], (N_EVEN, CONV_A_WIDTH, D_A), CONV_A_WIDTH ** -0.5),
        'db_a': nrm(ks[13], (N_EVEN, D_A), 0.02),
        'ln_a_g': 1.0 + nrm(ks[14], (N_EVEN, D_A), 0.05),
        'ln_a_b': nrm(ks[15], (N_EVEN, D_A), 0.02),
        'w_pool': nrm(ks[16], (N_EVEN, N_POOL_GROUPS, D_POOL_GROUP, D_POOL_GROUP), D_POOL_GROUP ** -0.5),
        'pool_scale': 1.0 + nrm(ks[17], (N_EVEN, D_B), 0.1),
        'w_out_even': nrm(ks[18], (N_EVEN, D_A + D_B, D_MODEL), (D_A + D_B) ** -0.5),
        'w_in_odd': nrm(ks[19], (N_ODD, D_MODEL, 3 * D_C), D_MODEL ** -0.5),
        'dw_c': nrm(ks[20], (N_ODD, CONV_C_WIDTH, D_C), CONV_C_WIDTH ** -0.5),
        'w_out_odd': nrm(ks[21], (N_ODD, D_C, D_MODEL), D_C ** -0.5),
        'peer_wq': nrm(ks[22], (DEPTH, D_MODEL, PEER_HEADS * PEER_DQ), D_MODEL ** -0.5),
        'peer_k1': nrm(ks[23], (DEPTH, PEER_HEADS, PEER_NKEYS, PEER_DHALF), PEER_DHALF ** -0.5),
        'peer_k2': nrm(ks[24], (DEPTH, PEER_HEADS, PEER_NKEYS, PEER_DHALF), PEER_DHALF ** -0.5),
        'peer_u': nrm(ks[25], (DEPTH, PEER_EXPERTS, D_MODEL), D_MODEL ** -0.5),
        'peer_v': nrm(ks[26], (DEPTH, PEER_EXPERTS, D_MODEL), PEER_HEADS ** -0.5),
        'final_g': 1.0 + nrm(ks[27], (D_MODEL,), 0.05),
    }


def reference(x_prompt, x_sample, state_conv_a, state_pool_b, state_conv_c, c_prompt, c_sample,
              norm_mix_g, norm_ffn_g, w_ada, b_ada, w_in_even, dw_a, db_a, ln_a_g, ln_a_b,
              w_pool, pool_scale, w_out_even, w_in_odd, dw_c, w_out_odd,
              peer_wq, peer_k1, peer_k2, peer_u, peer_v, final_g):
    bp = x_prompt.shape[0]
    dt = x_prompt.dtype
    zero_a = jnp.zeros((N_EVEN, bp, CONV_A_WIDTH - 1, D_A), dt)
    zero_b = jnp.zeros((N_EVEN, bp, POOL_BUF, D_B), dt)
    zero_c = jnp.zeros((N_ODD, bp, CONV_C_WIDTH - 1, D_C), dt)
    y_prompt, pa_a, pa_b, pa_c = run_trunk(
        x_prompt, c_prompt, zero_a, zero_b, zero_c, 0, norm_mix_g, norm_ffn_g, w_ada, b_ada,
        w_in_even, dw_a, db_a, ln_a_g, ln_a_b, w_pool, pool_scale, w_out_even,
        w_in_odd, dw_c, w_out_odd, peer_wq, peer_k1, peer_k2, peer_u, peer_v, final_g)
    y_sample, sa_a, sa_b, sa_c = run_trunk(
        x_sample, c_sample, state_conv_a, state_pool_b, state_conv_c, PAST_LEN, norm_mix_g, norm_ffn_g,
        w_ada, b_ada, w_in_even, dw_a, db_a, ln_a_g, ln_a_b, w_pool, pool_scale, w_out_even,
        w_in_odd, dw_c, w_out_odd, peer_wq, peer_k1, peer_k2, peer_u, peer_v, final_g)
    return (y_prompt, y_sample, pa_a, pa_b, pa_c, sa_a, sa_b, sa_c)
```

```python
import functools

import jax
import jax.numpy as jnp
from jax import lax
from jax.experimental import pallas as pl
from jax.experimental.pallas import tpu as pltpu

F32 = jnp.float32
BF16 = jnp.bfloat16

PAST_LEN = 4096
POOL_WINDOWS = (2, 4, 8, 16)
PEER_TOPK = 16
N_MOD = 6
EPS = 1e-6

LANES = 128
SUBLANES = 8
VMEM_LIMIT_CAP = 56 * 1024 * 1024

HIST_A = 32
HIST_B = 16
HIST_C = 8
CONV_ROW_CHUNK = 32


def _vmem_limit(nbytes):
    return int(min(VMEM_LIMIT_CAP, max(16 * 1024 * 1024, nbytes)))


def _token_tile(batch, seq, target):
    if seq >= target:
        assert seq % target == 0
        return 1, target
    nb = min(batch, target // seq)
    assert batch % nb == 0
    return nb, seq


def _ada_kernel(c_ref, w_ref, b_ref, o_ref):
    c = c_ref[...]
    ca = (c * jax.nn.sigmoid(c)).astype(BF16)
    o_ref[0] = jnp.dot(ca, w_ref[0].astype(BF16), preferred_element_type=F32) + b_ref[0]


def _ada_call(c_all, w_ada, b_ada):
    depth, d, n = w_ada.shape
    bt = c_all.shape[0]
    tn = 1024
    return pl.pallas_call(
        _ada_kernel,
        out_shape=jax.ShapeDtypeStruct((depth, bt, n), F32),
        grid=(depth, n // tn),
        in_specs=[
            pl.BlockSpec((bt, d), lambda l, j: (0, 0)),
            pl.BlockSpec((1, d, tn), lambda l, j: (l, 0, j)),
            pl.BlockSpec((1, 1, tn), lambda l, j: (l, 0, j)),
        ],
        out_specs=pl.BlockSpec((1, bt, tn), lambda l, j: (l, 0, j)),
        compiler_params=pltpu.CompilerParams(
            dimension_semantics=("arbitrary", "arbitrary"),
            vmem_limit_bytes=_vmem_limit(4 * d * tn * 4),
        ),
        name="ada_mod",
    )(c_all, w_ada, b_ada.reshape(depth, 1, n))


def _modulated_rmsnorm(x, g, shift, scale):
    y = x * lax.rsqrt(jnp.mean(x * x, axis=-1, keepdims=True) + EPS)
    return (y * g) * (1.0 + scale) + shift


def _gelu_tanh(x):
    c0 = 0.7978845608028654
    inner = x * (c0 + (c0 * 0.044715) * (x * x))
    return (0.5 * x) * (1.0 + jnp.tanh(inner))


def _mix_even_kernel(x_ref, mod_ref, pa_ref, pb_ref, g_ref, win_ref, dwa_ref, dba_ref, lng_ref,
                     lnb_ref, wpool_ref, pscale_ref, wout_ref,
                     xo_ref, na_ref, nbo_ref, exta_ref, extb_ref, *, nb, tt, pos0, n_tblk):
    t = pl.program_id(1)
    d_a = exta_ref.shape[-1]
    d_b = extb_ref.shape[-1]
    n_tap = dwa_ref.shape[0]
    rows = nb * tt

    @pl.when(t == 0)
    def _():
        exta_ref[:, 0:HIST_A - (n_tap - 1), :] = jnp.zeros((nb, HIST_A - (n_tap - 1), d_a), F32)
        exta_ref[:, HIST_A - (n_tap - 1):HIST_A, :] = pa_ref[...]
        extb_ref[:, 0:1, :] = jnp.zeros((nb, 1, d_b), F32)
        extb_ref[:, 1:HIST_B, :] = pb_ref[...]

    x = x_ref[...]
    m = mod_ref[...]
    h = _modulated_rmsnorm(x, g_ref[...], m[:, 0:1, :], m[:, 1:2, :])
    p = jnp.dot(h.reshape(rows, -1).astype(BF16), win_ref[...], preferred_element_type=F32)
    a = p[:, :d_a] * jax.nn.sigmoid(p[:, d_a:2 * d_a])
    b_in = p[:, 2 * d_a:]
    exta_ref[:, HIST_A:HIST_A + tt, :] = a.reshape(nb, tt, d_a)
    extb_ref[:, HIST_B:HIST_B + tt, :] = b_in.reshape(nb, tt, d_b)

    taps = [dwa_ref[k:k + 1, :] for k in range(n_tap)]
    first = HIST_A - (n_tap - 1)
    pieces = []
    for bi in range(nb):
        for r in range(0, tt, CONV_ROW_CHUNK):
            rc = min(CONV_ROW_CHUNK, tt - r)
            acc = taps[0] * exta_ref[bi, first + r:first + r + rc, :]
            for k in range(1, n_tap):
                acc = acc + taps[k] * exta_ref[bi, first + r + k:first + r + k + rc, :]
            pieces.append(acc)
    conv = jnp.concatenate(pieces, axis=0) + dba_ref[...]
    mu = jnp.mean(conv, axis=-1, keepdims=True)
    cen = conv - mu
    var = jnp.mean(cen * cen, axis=-1, keepdims=True)
    ln = (cen * lax.rsqrt(var + EPS)) * lng_ref[...] + lnb_ref[...]
    a_out = ln * jax.nn.sigmoid(ln)

    dg = d_b // len(POOL_WINDOWS)
    pos = pos0 + t * tt + lax.broadcasted_iota(jnp.int32, (nb, tt, dg), 1)
    b_groups = []
    for gi, w in enumerate(POOL_WINDOWS):
        sl = slice(gi * dg, (gi + 1) * dg)
        s = extb_ref[:, HIST_B:HIST_B + tt, sl]
        for k in range(1, w):
            s = s + extb_ref[:, HIST_B - k:HIST_B - k + tt, sl]
        cnt = jnp.minimum(w, pos + 1).astype(F32)
        pooled = s / cnt - extb_ref[:, HIST_B:HIST_B + tt, sl]
        b_groups.append(jnp.dot(pooled.reshape(rows, dg).astype(BF16), wpool_ref[gi],
                                preferred_element_type=F32))
    b_out = jnp.concatenate(b_groups, axis=-1) * pscale_ref[...]

    ab = jnp.concatenate([a_out, b_out], axis=-1).astype(BF16)
    y = jnp.dot(ab, wout_ref[...], preferred_element_type=F32)
    xo_ref[...] = x + m[:, 2:3, :] * y.reshape(nb, tt, -1)

    @pl.when(t == n_tblk - 1)
    def _():
        na_ref[...] = exta_ref[:, HIST_A + tt - (n_tap - 1):HIST_A + tt, :]
        nbo_ref[...] = extb_ref[:, HIST_B + tt - (HIST_B - 1):HIST_B + tt, :]

    exta_ref[:, 0:HIST_A, :] = exta_ref[:, tt:tt + HIST_A, :]
    extb_ref[:, 0:HIST_B, :] = extb_ref[:, tt:tt + HIST_B, :]


def _mix_even_call(x, mod, prev_a, prev_b, g_mix, w_in, dw_a, db_a, ln_g, ln_b, w_pool, pool_scale,
                   w_out, *, pos0, tile_tokens):
    b, t, d = x.shape
    nb, tt = _token_tile(b, t, tile_tokens)
    n_tblk = t // tt
    d_a = prev_a.shape[-1]
    d_b = prev_b.shape[-1]
    n_tap = dw_a.shape[0]
    assert n_tap - 1 <= HIST_A and prev_b.shape[1] == HIST_B - 1 and tt >= HIST_A
    const2 = lambda i, j: (0, 0)
    kern = functools.partial(_mix_even_kernel, nb=nb, tt=tt, pos0=pos0, n_tblk=n_tblk)
    vmem = (4 * nb * tt * d * 4 + 2 * (w_in.size + w_out.size) * 2
            + nb * (HIST_A + HIST_B + 2 * tt) * d_a * 4 + 12 * nb * tt * d * 4)
    return pl.pallas_call(
        kern,
        out_shape=(jax.ShapeDtypeStruct(x.shape, F32),
                   jax.ShapeDtypeStruct(prev_a.shape, F32),
                   jax.ShapeDtypeStruct(prev_b.shape, F32)),
        grid=(b // nb, n_tblk),
        in_specs=[
            pl.BlockSpec((nb, tt, d), lambda i, j: (i, j, 0)),
            pl.BlockSpec((nb, N_MOD, d), lambda i, j: (i, 0, 0)),
            pl.BlockSpec((nb, n_tap - 1, d_a), lambda i, j: (i, 0, 0)),
            pl.BlockSpec((nb, HIST_B - 1, d_b), lambda i, j: (i, 0, 0)),
            pl.BlockSpec((1, d), const2),
            pl.BlockSpec(w_in.shape, const2),
            pl.BlockSpec(dw_a.shape, const2),
            pl.BlockSpec((1, d_a), const2),
            pl.BlockSpec((1, d_a), const2),
            pl.BlockSpec((1, d_a), const2),
            pl.BlockSpec(w_pool.shape, lambda i, j: (0, 0, 0)),
            pl.BlockSpec((1, d_b), const2),
            pl.BlockSpec(w_out.shape, const2),
        ],
        out_specs=(
            pl.BlockSpec((nb, tt, d), lambda i, j: (i, j, 0)),
            pl.BlockSpec((nb, n_tap - 1, d_a), lambda i, j: (i, 0, 0)),
            pl.BlockSpec((nb, HIST_B - 1, d_b), lambda i, j: (i, 0, 0)),
        ),
        scratch_shapes=[pltpu.VMEM((nb, HIST_A + tt, d_a), F32),
                        pltpu.VMEM((nb, HIST_B + tt, d_b), F32)],
        compiler_params=pltpu.CompilerParams(
            dimension_semantics=("arbitrary", "arbitrary"),
            vmem_limit_bytes=_vmem_limit(vmem),
        ),
        name="mix_even",
    )(x, mod, prev_a, prev_b, g_mix.reshape(1, d), w_in, dw_a, db_a.reshape(1, d_a),
      ln_g.reshape(1, d_a), ln_b.reshape(1, d_a), w_pool, pool_scale.reshape(1, d_b), w_out)


def _mix_odd_kernel(x_ref, mod_ref, pc_ref, g_ref, win_ref, dwc_ref, wout_ref,
                    xo_ref, nc_ref, ext_ref, *, nb, tt, n_tblk):
    t = pl.program_id(1)
    d_c = ext_ref.shape[-1]
    n_tap = dwc_ref.shape[0]
    rows = nb * tt
    first = HIST_C - (n_tap - 1)

    @pl.when(t == 0)
    def _():
        ext_ref[:, 0:first, :] = jnp.zeros((nb, first, d_c), F32)
        ext_ref[:, first:HIST_C, :] = pc_ref[...]

    x = x_ref[...]
    m = mod_ref[...]
    h = _modulated_rmsnorm(x, g_ref[...], m[:, 0:1, :], m[:, 1:2, :])
    p = jnp.dot(h.reshape(rows, -1).astype(BF16), win_ref[...], preferred_element_type=F32)
    bg = p[:, :d_c]
    z = p[:, d_c:2 * d_c] * p[:, 2 * d_c:]
    ext_ref[:, HIST_C:HIST_C + tt, :] = z.reshape(nb, tt, d_c)
    conv = dwc_ref[0:1, :] * ext_ref[:, first:first + tt, :]
    for k in range(1, n_tap):
        conv = conv + dwc_ref[k:k + 1, :] * ext_ref[:, first + k:first + k + tt, :]
    gated = (bg * conv.reshape(rows, d_c)).astype(BF16)
    y = jnp.dot(gated, wout_ref[...], preferred_element_type=F32)
    xo_ref[...] = x + m[:, 2:3, :] * y.reshape(nb, tt, -1)

    @pl.when(t == n_tblk - 1)
    def _():
        nc_ref[...] = ext_ref[:, HIST_C + tt - (n_tap - 1):HIST_C + tt, :]

    ext_ref[:, 0:HIST_C, :] = ext_ref[:, tt:tt + HIST_C, :]


def _mix_odd_call(x, mod, prev_c, g_mix, w_in, dw_c, w_out, *, tile_tokens):
    b, t, d = x.shape
    nb, tt = _token_tile(b, t, tile_tokens)
    n_tblk = t // tt
    d_c = prev_c.shape[-1]
    n_tap = dw_c.shape[0]
    assert n_tap - 1 <= HIST_C and tt >= HIST_C
    const2 = lambda i, j: (0, 0)
    kern = functools.partial(_mix_odd_kernel, nb=nb, tt=tt, n_tblk=n_tblk)
    vmem = (4 * nb * tt * d * 4 + 2 * (w_in.size + w_out.size) * 2
            + nb * (HIST_C + tt) * d_c * 4 + 12 * nb * tt * d * 4)
    return pl.pallas_call(
        kern,
        out_shape=(jax.ShapeDtypeStruct(x.shape, F32), jax.ShapeDtypeStruct(prev_c.shape, F32)),
        grid=(b // nb, n_tblk),
        in_specs=[
            pl.BlockSpec((nb, tt, d), lambda i, j: (i, j, 0)),
            pl.BlockSpec((nb, N_MOD, d), lambda i, j: (i, 0, 0)),
            pl.BlockSpec((nb, n_tap - 1, d_c), lambda i, j: (i, 0, 0)),
            pl.BlockSpec((1, d), const2),
            pl.BlockSpec(w_in.shape, const2),
            pl.BlockSpec(dw_c.shape, const2),
            pl.BlockSpec(w_out.shape, const2),
        ],
        out_specs=(
            pl.BlockSpec((nb, tt, d), lambda i, j: (i, j, 0)),
            pl.BlockSpec((nb, n_tap - 1, d_c), lambda i, j: (i, 0, 0)),
        ),
        scratch_shapes=[pltpu.VMEM((nb, HIST_C + tt, d_c), F32)],
        compiler_params=pltpu.CompilerParams(
            dimension_semantics=("arbitrary", "arbitrary"),
            vmem_limit_bytes=_vmem_limit(vmem),
        ),
        name="mix_odd",
    )(x, mod, prev_c, g_mix.reshape(1, d), w_in, dw_c, w_out)


def _hi(a, b):
    if a is None:
        return b
    if b is None:
        return a
    return jnp.maximum(a, b)


def _lo(a, b):
    if a is None or b is None:
        return None
    return jnp.minimum(a, b)


def _bitonic_merge_desc(xs):
    n = len(xs)
    xs = list(xs)
    j = n // 2
    while j >= 1:
        for i in range(n):
            l = i ^ j
            if l > i:
                xs[i], xs[l] = _hi(xs[i], xs[l]), _lo(xs[i], xs[l])
        j //= 2
    return xs


def _bitonic_sort_desc(xs):
    n = len(xs)
    xs = list(xs)
    k = 2
    while k <= n:
        j = k // 2
        while j >= 1:
            for i in range(n):
                l = i ^ j
                if l > i:
                    hi, lo = _hi(xs[i], xs[l]), _lo(xs[i], xs[l])
                    if (i & k) == 0 or k == n:
                        xs[i], xs[l] = hi, lo
                    else:
                        xs[i], xs[l] = lo, hi
            j //= 2
        k *= 2
    return xs


def _merge_top(a, b):
    n = len(a)
    return _bitonic_merge_desc([_hi(a[k], b[n - 1 - k]) for k in range(n)])


def _top_sorted(xs, k):
    xs = list(xs) + [None] * ((-len(xs)) % k)
    groups = [_bitonic_sort_desc(xs[i:i + k]) for i in range(0, len(xs), k)]
    while len(groups) > 1:
        nxt = [_merge_top(groups[i], groups[i + 1]) for i in range(0, len(groups) - 1, 2)]
        if len(groups) % 2:
            nxt.append(groups[-1])
        groups = nxt
    return groups[0]


def _top_over_sublanes(vregs, k):
    xs = _bitonic_sort_desc(vregs)
    shift = SUBLANES // 2
    while shift >= 1:
        rolled = [pltpu.roll(x, shift, 0) for x in xs]
        xs = _merge_top(xs, rolled)
        shift //= 2
    return xs


def _peer_prep_kernel(x_ref, mod_ref, g_ref, wqt_ref, k1_ref, k2_ref,
                      ht_ref, e1n_ref, psi_ref, e2_ref, s1_ref, s2_ref, *, nb, tt):
    rows = nb * tt
    n_heads, n_keys, d_half = k1_ref.shape
    x = x_ref[...]
    m = mod_ref[...]
    h = _modulated_rmsnorm(x, g_ref[...], m[:, 3:4, :], m[:, 4:5, :]).reshape(rows, -1)
    ht = h.T.astype(BF16)
    ht_ref[...] = ht
    qt = jnp.dot(wqt_ref[...], ht, preferred_element_type=F32).astype(BF16)
    for hd in range(n_heads):
        base = hd * 2 * d_half
        s1_ref[hd] = jnp.dot(k1_ref[hd], qt[base:base + d_half], preferred_element_type=F32)
        s2_ref[hd] = jnp.dot(k2_ref[hd], qt[base + d_half:base + 2 * d_half],
                             preferred_element_type=F32)

    n_vreg = n_keys // SUBLANES
    assert n_vreg == PEER_TOPK and n_heads == SUBLANES
    sub = lax.broadcasted_iota(jnp.int32, (SUBLANES, LANES), 0)
    pairs = [(a, b) for a in range(PEER_TOPK) for b in range(PEER_TOPK)
             if (a + 1) * (b + 1) <= PEER_TOPK]

    def lane_group(lg, carry):
        lanes = pl.ds(pl.multiple_of(lg * LANES, LANES), LANES)
        v1 = [jnp.zeros((SUBLANES, LANES), F32)] * PEER_TOPK
        v2 = [jnp.zeros((SUBLANES, LANES), F32)] * PEER_TOPK
        for hd in range(n_heads):
            t1 = _top_over_sublanes(
                [s1_ref[hd, v * SUBLANES:(v + 1) * SUBLANES, lanes] for v in range(n_vreg)], PEER_TOPK)
            t2 = _top_over_sublanes(
                [s2_ref[hd, v * SUBLANES:(v + 1) * SUBLANES, lanes] for v in range(n_vreg)], PEER_TOPK)
            v1 = [jnp.where(sub == hd, t1[a], v1[a]) for a in range(PEER_TOPK)]
            v2 = [jnp.where(sub == hd, t2[a], v2[a]) for a in range(PEER_TOPK)]
        top = _top_sorted([v1[a] + v2[b] for a, b in pairs], PEER_TOPK)
        tau = top[PEER_TOPK - 1]
        z = jnp.ones((SUBLANES, LANES), F32)
        for k in range(1, PEER_TOPK):
            z = z + jnp.exp(top[k] - top[0])
        inv_z = 1.0 / z
        tau_m2 = tau - v2[0]
        for hd in range(n_heads):
            row = lambda arr: jnp.broadcast_to(arr[hd:hd + 1, :], (SUBLANES, LANES))
            m1, m2, tm, iz = row(v1[0]), row(v2[0]), row(tau_m2), row(inv_z)
            for v in range(n_vreg):
                rs = slice(v * SUBLANES, (v + 1) * SUBLANES)
                s1v = s1_ref[hd, rs, lanes]
                e1n_ref[hd, rs, lanes] = jnp.exp(s1v - m1) * iz
                psi_ref[hd, rs, lanes] = jnp.exp(tm - s1v)
                e2_ref[hd, rs, lanes] = jnp.exp(s2_ref[hd, rs, lanes] - m2)
        return carry

    lax.fori_loop(0, rows // LANES, lane_group, 0)


def _peer_prep_call(x, mod, g_ffn, wqt, k1, k2, *, tile_tokens):
    b, t, d = x.shape
    nb, tt = _token_tile(b, t, tile_tokens)
    rows = nb * tt
    n_tb = t // tt
    ntok = b * t
    n_heads, n_keys, _ = k1.shape
    kern = functools.partial(_peer_prep_kernel, nb=nb, tt=tt)
    blk = lambda i: (i // n_tb, i % n_tb, 0)
    e_shape = jax.ShapeDtypeStruct((n_heads, n_keys, ntok), F32)
    e_spec = pl.BlockSpec((n_heads, n_keys, rows), lambda i: (0, 0, i))
    vmem = (2 * rows * d * 4 + 2 * wqt.size * 2 + 2 * rows * d * 2 + 8 * n_heads * n_keys * rows * 4
            + 6 * rows * d * 4 + wqt.shape[0] * rows * 6)
    return pl.pallas_call(
        kern,
        out_shape=(jax.ShapeDtypeStruct((d, ntok), BF16), e_shape, e_shape, e_shape),
        grid=(ntok // rows,),
        in_specs=[
            pl.BlockSpec((nb, tt, d), blk),
            pl.BlockSpec((nb, N_MOD, d), lambda i: (i // n_tb, 0, 0)),
            pl.BlockSpec((1, d), lambda i: (0, 0)),
            pl.BlockSpec(wqt.shape, lambda i: (0, 0)),
            pl.BlockSpec(k1.shape, lambda i: (0, 0, 0)),
            pl.BlockSpec(k2.shape, lambda i: (0, 0, 0)),
        ],
        out_specs=(pl.BlockSpec((d, rows), lambda i: (0, i)), e_spec, e_spec, e_spec),
        scratch_shapes=[pltpu.VMEM((n_heads, n_keys, rows), F32),
                        pltpu.VMEM((n_heads, n_keys, rows), F32)],
        compiler_params=pltpu.CompilerParams(
            dimension_semantics=("arbitrary",),
            vmem_limit_bytes=_vmem_limit(vmem),
        ),
        name="peer_prep",
    )(x, mod, g_ffn.reshape(1, d), wqt, k1, k2)


def _peer_main_kernel(x_ref, mod_ref, ht_ref, e1n_ref, psi_ref, e2_ref, u_ref, vt_ref, fg_ref,
                      o_ref, act_ref, prod_ref, acc_ref, *, nb, tt, n_chunks, total, final_norm):
    s = pl.program_id(0)
    n_heads = e2_ref.shape[0]
    n_keys = e2_ref.shape[1]
    rows = nb * tt
    e_c = u_ref.shape[0]
    s3 = jnp.clip(s - 2, 0, total - 1)
    c3 = s3 % n_chunks

    @pl.when(s == 0)
    def _():
        act_ref[...] = jnp.zeros(act_ref.shape, F32)
        prod_ref[...] = jnp.zeros(prod_ref.shape, BF16)

    @pl.when(c3 == 0)
    def _():
        acc_ref[...] = jnp.zeros(acc_ref.shape, F32)

    slot1 = s % 2
    slot2 = (s + 1) % 2
    act_ref[slot1] = jnp.dot(u_ref[...], ht_ref[...], preferred_element_type=F32)
    acc_ref[...] += jnp.dot(vt_ref[...], prod_ref[slot1], preferred_element_type=F32)
    for ii in range(e_c // n_keys):
        er = slice(ii * n_keys, (ii + 1) * n_keys)
        for lg in range(rows // LANES):
            lanes = slice(lg * LANES, (lg + 1) * LANES)
            w = None
            for hd in range(n_heads):
                e1r = e1n_ref[hd, ii:ii + 1, lanes]
                psr = psi_ref[hd, ii:ii + 1, lanes]
                e2t = e2_ref[hd, :, lanes]
                term = jnp.where(e2t >= psr, e2t, 0.0) * e1r
                w = term if w is None else w + term
            a = act_ref[slot2, er, lanes]
            prod_ref[slot2, er, lanes] = (_gelu_tanh(a) * w).astype(BF16)

    @pl.when(jnp.logical_and(c3 == n_chunks - 1, s >= 2))
    def _():
        m = mod_ref[...]
        y = x_ref[...] + m[:, 5:6, :] * acc_ref[...].T.reshape(nb, tt, -1)
        if final_norm:
            y = (y * lax.rsqrt(jnp.mean(y * y, axis=-1, keepdims=True) + EPS)) * fg_ref[...]
        o_ref[...] = y


def _peer_main_call(x, mod, ht, e1n, psi, e2, u_bf, vt_bf, final_g, *, tile_tokens, chunk_experts,
                    final_norm):
    b, t, d = x.shape
    nb, tt = _token_tile(b, t, tile_tokens)
    rows = nb * tt
    n_tb = t // tt
    ntok = b * t
    n_blocks = ntok // rows
    n_exp = u_bf.shape[0]
    n_heads, n_keys, _ = e2.shape
    e_c = chunk_experts
    n_chunks = n_exp // e_c
    n_i = e_c // n_keys
    assert n_i % SUBLANES == 0 and n_chunks > 1
    total = n_blocks * n_chunks
    last = total - 1
    st1 = lambda s: jnp.minimum(s, last)
    st2 = lambda s: jnp.clip(s - 1, 0, last)
    st3 = lambda s: jnp.clip(s - 2, 0, last)
    xblk = lambda s: ((st3(s) // n_chunks) // n_tb, (st3(s) // n_chunks) % n_tb, 0)
    kern = functools.partial(_peer_main_kernel, nb=nb, tt=tt, n_chunks=n_chunks, total=total,
                             final_norm=final_norm)
    vmem = (4 * rows * d * 4 + 2 * d * rows * 2 + 2 * n_heads * n_keys * rows * 4
            + 4 * n_heads * n_i * rows * 4 + 4 * e_c * d * 2 + 2 * e_c * rows * 4
            + 2 * e_c * rows * 2 + d * rows * 4 + 3 * rows * d * 4)
    return pl.pallas_call(
        kern,
        out_shape=jax.ShapeDtypeStruct(x.shape, F32),
        grid=(total + 2,),
        in_specs=[
            pl.BlockSpec((nb, tt, d), xblk),
            pl.BlockSpec((nb, N_MOD, d), lambda s: ((st3(s) // n_chunks) // n_tb, 0, 0)),
            pl.BlockSpec((d, rows), lambda s: (0, st1(s) // n_chunks)),
            pl.BlockSpec((n_heads, n_i, rows), lambda s: (0, st2(s) % n_chunks, st2(s) // n_chunks)),
            pl.BlockSpec((n_heads, n_i, rows), lambda s: (0, st2(s) % n_chunks, st2(s) // n_chunks)),
            pl.BlockSpec((n_heads, n_keys, rows), lambda s: (0, 0, st2(s) // n_chunks)),
            pl.BlockSpec((e_c, d), lambda s: (st1(s) % n_chunks, 0)),
            pl.BlockSpec((d, e_c), lambda s: (0, st3(s) % n_chunks)),
            pl.BlockSpec((1, d), lambda s: (0, 0)),
        ],
        out_specs=pl.BlockSpec((nb, tt, d), xblk),
        scratch_shapes=[pltpu.VMEM((2, e_c, rows), F32),
                        pltpu.VMEM((2, e_c, rows), BF16),
                        pltpu.VMEM((d, rows), F32)],
        compiler_params=pltpu.CompilerParams(
            dimension_semantics=("arbitrary",),
            vmem_limit_bytes=_vmem_limit(vmem),
        ),
        name="peer_main",
    )(x, mod, ht, e1n, psi, e2, u_bf, vt_bf, final_g.reshape(1, d))


MIX_TILE_TOKENS = 256
PREP_TILE_TOKENS = 256
MAIN_TILE_TOKENS = 512
MAIN_CHUNK_EXPERTS = 1024


def _run_trunk(x, mod, prev_a, prev_b, prev_c, pos0, wts):
    depth = mod.shape[0]
    new_a, new_b, new_c = [], [], []
    for l in range(depth):
        i = l // 2
        if l % 2 == 0:
            x, sa, sb = _mix_even_call(
                x, mod[l], prev_a[i], prev_b[i], wts["norm_mix_g"][l], wts["w_in_even"][i],
                wts["dw_a"][i], wts["db_a"][i], wts["ln_a_g"][i], wts["ln_a_b"][i], wts["w_pool"][i],
                wts["pool_scale"][i], wts["w_out_even"][i], pos0=pos0, tile_tokens=MIX_TILE_TOKENS)
            new_a.append(sa)
            new_b.append(sb)
        else:
            x, sc = _mix_odd_call(x, mod[l], prev_c[i], wts["norm_mix_g"][l], wts["w_in_odd"][i],
                                  wts["dw_c"][i], wts["w_out_odd"][i], tile_tokens=MIX_TILE_TOKENS)
            new_c.append(sc)
        ht, e1n, psi, e2 = _peer_prep_call(x, mod[l], wts["norm_ffn_g"][l], wts["peer_wqt"][l],
                                           wts["peer_k1"][l], wts["peer_k2"][l],
                                           tile_tokens=PREP_TILE_TOKENS)
        x = _peer_main_call(x, mod[l], ht, e1n, psi, e2, wts["peer_u"][l], wts["peer_vt"][l],
                            wts["final_g"], tile_tokens=MAIN_TILE_TOKENS,
                            chunk_experts=MAIN_CHUNK_EXPERTS, final_norm=(l == depth - 1))
    return x, jnp.stack(new_a), jnp.stack(new_b), jnp.stack(new_c)


def kernel(x_prompt, x_sample, state_conv_a, state_pool_b, state_conv_c, c_prompt, c_sample, norm_mix_g, norm_ffn_g, w_ada, b_ada, w_in_even, dw_a, db_a, ln_a_g, ln_a_b, w_pool, pool_scale, w_out_even, w_in_odd, dw_c, w_out_odd, peer_wq, peer_k1, peer_k2, peer_u, peer_v, final_g):
    bp, _, d = x_prompt.shape
    bs = x_sample.shape[0]
    depth = w_ada.shape[0]
    wts = dict(
        norm_mix_g=norm_mix_g, norm_ffn_g=norm_ffn_g, final_g=final_g,
        w_in_even=w_in_even.astype(BF16), dw_a=dw_a, db_a=db_a, ln_a_g=ln_a_g, ln_a_b=ln_a_b,
        w_pool=w_pool.astype(BF16), pool_scale=pool_scale, w_out_even=w_out_even.astype(BF16),
        w_in_odd=w_in_odd.astype(BF16), dw_c=dw_c, w_out_odd=w_out_odd.astype(BF16),
        peer_wqt=jnp.swapaxes(peer_wq, 1, 2).astype(BF16),
        peer_k1=peer_k1.astype(BF16), peer_k2=peer_k2.astype(BF16),
        peer_u=peer_u.astype(BF16), peer_vt=jnp.swapaxes(peer_v, 1, 2).astype(BF16),
    )
    mod = _ada_call(jnp.concatenate([c_prompt, c_sample], axis=0), w_ada, b_ada)
    mod = mod.reshape(depth, bp + bs, N_MOD, d)
    n_even, _, ha, d_a = state_conv_a.shape
    _, _, hb, d_b = state_pool_b.shape
    n_odd, _, hc, d_c = state_conv_c.shape
    y_p, pa_a, pa_b, pa_c = _run_trunk(
        x_prompt, mod[:, :bp], jnp.zeros((n_even, bp, ha, d_a), F32),
        jnp.zeros((n_even, bp, hb, d_b), F32), jnp.zeros((n_odd, bp, hc, d_c), F32), 0, wts)
    y_s, sa_a, sa_b, sa_c = _run_trunk(
        x_sample, mod[:, bp:], state_conv_a, state_pool_b, state_conv_c, PAST_LEN, wts)
    return (y_p, y_s, pa_a, pa_b, pa_c, sa_a, sa_b, sa_c)
```

```python
import functools

import jax
import jax.numpy as jnp
from jax import lax
from jax.experimental import pallas as pl
from jax.experimental.pallas import tpu as pltpu

F32 = jnp.float32
BF16 = jnp.bfloat16

PAST_LEN = 4096
POOL_WINDOWS = (2, 4, 8, 16)
PEER_TOPK = 16
N_MOD = 6
EPS = 1e-6

LANES = 128
SUBLANES = 8
VMEM_LIMIT_CAP = 56 * 1024 * 1024

HIST_A = 32
HIST_B = 16
HIST_C = 8
CONV_ROW_CHUNK = 32
MXU_PIECE_ROWS = 256


def _vmem_limit(nbytes):
    return int(min(VMEM_LIMIT_CAP, max(16 * 1024 * 1024, nbytes)))


def _token_tile(batch, seq, target):
    if seq >= target:
        assert seq % target == 0
        return 1, target
    nb = min(batch, target // seq)
    assert batch % nb == 0
    return nb, seq


def _ada_kernel(c_ref, w_ref, b_ref, o_ref):
    c = c_ref[...]
    ca = (c * jax.nn.sigmoid(c)).astype(BF16)
    o_ref[0] = jnp.dot(ca, w_ref[0].astype(BF16), preferred_element_type=F32) + b_ref[0]


def _ada_call(c_all, w_ada, b_ada):
    depth, d, n = w_ada.shape
    bt = c_all.shape[0]
    tn = 1024
    return pl.pallas_call(
        _ada_kernel,
        out_shape=jax.ShapeDtypeStruct((depth, bt, n), F32),
        grid=(depth, n // tn),
        in_specs=[
            pl.BlockSpec((bt, d), lambda l, j: (0, 0)),
            pl.BlockSpec((1, d, tn), lambda l, j: (l, 0, j)),
            pl.BlockSpec((1, 1, tn), lambda l, j: (l, 0, j)),
        ],
        out_specs=pl.BlockSpec((1, bt, tn), lambda l, j: (l, 0, j)),
        compiler_params=pltpu.CompilerParams(
            dimension_semantics=("arbitrary", "arbitrary"),
            vmem_limit_bytes=_vmem_limit(4 * d * tn * 4),
        ),
        name="ada_mod",
    )(c_all, w_ada, b_ada.reshape(depth, 1, n))


def _modulated_rmsnorm(x, g, shift, scale):
    y = x * lax.rsqrt(jnp.mean(x * x, axis=-1, keepdims=True) + EPS)
    return (y * g) * (1.0 + scale) + shift


def _gelu_tanh(x):
    c0 = 0.7978845608028654
    inner = x * (c0 + (c0 * 0.044715) * (x * x))
    return (0.5 * x) * (1.0 + jnp.tanh(inner))


def _mix_even_kernel(x_ref, mod_ref, pa_ref, pb_ref, g_ref, win_ref, dwa_ref, dba_ref, lng_ref,
                     lnb_ref, wpool_ref, pscale_ref, wout_ref,
                     xo_ref, na_ref, nbo_ref, exta_ref, extb_ref, *, nb, tt, pos0, n_tblk):
    t = pl.program_id(1)
    d_a = exta_ref.shape[-1]
    d_b = extb_ref.shape[-1]
    n_tap = dwa_ref.shape[0]
    rows = nb * tt

    @pl.when(t == 0)
    def _():
        exta_ref[:, 0:HIST_A - (n_tap - 1), :] = jnp.zeros((nb, HIST_A - (n_tap - 1), d_a), F32)
        exta_ref[:, HIST_A - (n_tap - 1):HIST_A, :] = pa_ref[...]
        extb_ref[:, 0:1, :] = jnp.zeros((nb, 1, d_b), F32)
        extb_ref[:, 1:HIST_B, :] = pb_ref[...]

    x = x_ref[...]
    m = mod_ref[...]
    h = _modulated_rmsnorm(x, g_ref[...], m[:, 0:1, :], m[:, 1:2, :])
    p = jnp.dot(h.reshape(rows, -1).astype(BF16), win_ref[...], preferred_element_type=F32)
    a = p[:, :d_a] * jax.nn.sigmoid(p[:, d_a:2 * d_a])
    b_in = p[:, 2 * d_a:]
    exta_ref[:, HIST_A:HIST_A + tt, :] = a.reshape(nb, tt, d_a)
    extb_ref[:, HIST_B:HIST_B + tt, :] = b_in.reshape(nb, tt, d_b)

    taps = [dwa_ref[k:k + 1, :] for k in range(n_tap)]
    first = HIST_A - (n_tap - 1)
    pieces = []
    for bi in range(nb):
        for r in range(0, tt, CONV_ROW_CHUNK):
            rc = min(CONV_ROW_CHUNK, tt - r)
            acc = taps[0] * exta_ref[bi, first + r:first + r + rc, :]
            for k in range(1, n_tap):
                acc = acc + taps[k] * exta_ref[bi, first + r + k:first + r + k + rc, :]
            pieces.append(acc)
    conv = jnp.concatenate(pieces, axis=0) + dba_ref[...]
    mu = jnp.mean(conv, axis=-1, keepdims=True)
    cen = conv - mu
    var = jnp.mean(cen * cen, axis=-1, keepdims=True)
    ln = (cen * lax.rsqrt(var + EPS)) * lng_ref[...] + lnb_ref[...]
    a_out = ln * jax.nn.sigmoid(ln)

    dg = d_b // len(POOL_WINDOWS)
    pos = pos0 + t * tt + lax.broadcasted_iota(jnp.int32, (nb, tt, dg), 1)
    b_groups = []
    for gi, w in enumerate(POOL_WINDOWS):
        sl = slice(gi * dg, (gi + 1) * dg)
        s = extb_ref[:, HIST_B:HIST_B + tt, sl]
        for k in range(1, w):
            s = s + extb_ref[:, HIST_B - k:HIST_B - k + tt, sl]
        cnt = jnp.minimum(w, pos + 1).astype(F32)
        pooled = s / cnt - extb_ref[:, HIST_B:HIST_B + tt, sl]
        b_groups.append(jnp.dot(pooled.reshape(rows, dg).astype(BF16), wpool_ref[gi],
                                preferred_element_type=F32))
    b_out = jnp.concatenate(b_groups, axis=-1) * pscale_ref[...]

    ab = jnp.concatenate([a_out, b_out], axis=-1).astype(BF16)
    y = jnp.dot(ab, wout_ref[...], preferred_element_type=F32)
    xo_ref[...] = x + m[:, 2:3, :] * y.reshape(nb, tt, -1)

    @pl.when(t == n_tblk - 1)
    def _():
        na_ref[...] = exta_ref[:, HIST_A + tt - (n_tap - 1):HIST_A + tt, :]
        nbo_ref[...] = extb_ref[:, HIST_B + tt - (HIST_B - 1):HIST_B + tt, :]

    exta_ref[:, 0:HIST_A, :] = exta_ref[:, tt:tt + HIST_A, :]
    extb_ref[:, 0:HIST_B, :] = extb_ref[:, tt:tt + HIST_B, :]


def _mix_even_call(x, mod, prev_a, prev_b, g_mix, w_in, dw_a, db_a, ln_g, ln_b, w_pool, pool_scale,
                   w_out, *, pos0, tile_tokens):
    b, t, d = x.shape
    nb, tt = _token_tile(b, t, tile_tokens)
    n_tblk = t // tt
    d_a = prev_a.shape[-1]
    d_b = prev_b.shape[-1]
    n_tap = dw_a.shape[0]
    assert n_tap - 1 <= HIST_A and prev_b.shape[1] == HIST_B - 1 and tt >= HIST_A
    const2 = lambda i, j: (0, 0)
    kern = functools.partial(_mix_even_kernel, nb=nb, tt=tt, pos0=pos0, n_tblk=n_tblk)
    vmem = (4 * nb * tt * d * 4 + 2 * (w_in.size + w_out.size) * 2
            + nb * (HIST_A + HIST_B + 2 * tt) * d_a * 4 + 12 * nb * tt * d * 4)
    return pl.pallas_call(
        kern,
        out_shape=(jax.ShapeDtypeStruct(x.shape, F32),
                   jax.ShapeDtypeStruct(prev_a.shape, F32),
                   jax.ShapeDtypeStruct(prev_b.shape, F32)),
        grid=(b // nb, n_tblk),
        in_specs=[
            pl.BlockSpec((nb, tt, d), lambda i, j: (i, j, 0)),
            pl.BlockSpec((nb, N_MOD, d), lambda i, j: (i, 0, 0)),
            pl.BlockSpec((nb, n_tap - 1, d_a), lambda i, j: (i, 0, 0)),
            pl.BlockSpec((nb, HIST_B - 1, d_b), lambda i, j: (i, 0, 0)),
            pl.BlockSpec((1, d), const2),
            pl.BlockSpec(w_in.shape, const2),
            pl.BlockSpec(dw_a.shape, const2),
            pl.BlockSpec((1, d_a), const2),
            pl.BlockSpec((1, d_a), const2),
            pl.BlockSpec((1, d_a), const2),
            pl.BlockSpec(w_pool.shape, lambda i, j: (0, 0, 0)),
            pl.BlockSpec((1, d_b), const2),
            pl.BlockSpec(w_out.shape, const2),
        ],
        out_specs=(
            pl.BlockSpec((nb, tt, d), lambda i, j: (i, j, 0)),
            pl.BlockSpec((nb, n_tap - 1, d_a), lambda i, j: (i, 0, 0)),
            pl.BlockSpec((nb, HIST_B - 1, d_b), lambda i, j: (i, 0, 0)),
        ),
        scratch_shapes=[pltpu.VMEM((nb, HIST_A + tt, d_a), F32),
                        pltpu.VMEM((nb, HIST_B + tt, d_b), F32)],
        compiler_params=pltpu.CompilerParams(
            dimension_semantics=("arbitrary", "arbitrary"),
            vmem_limit_bytes=_vmem_limit(vmem),
        ),
        name="mix_even",
    )(x, mod, prev_a, prev_b, g_mix.reshape(1, d), w_in, dw_a, db_a.reshape(1, d_a),
      ln_g.reshape(1, d_a), ln_b.reshape(1, d_a), w_pool, pool_scale.reshape(1, d_b), w_out)


def _mix_odd_kernel(x_ref, mod_ref, pc_ref, g_ref, win_ref, dwc_ref, wout_ref,
                    xo_ref, nc_ref, ext_ref, *, nb, tt, n_tblk):
    t = pl.program_id(1)
    d_c = ext_ref.shape[-1]
    n_tap = dwc_ref.shape[0]
    rows = nb * tt
    first = HIST_C - (n_tap - 1)

    @pl.when(t == 0)
    def _():
        ext_ref[:, 0:first, :] = jnp.zeros((nb, first, d_c), F32)
        ext_ref[:, first:HIST_C, :] = pc_ref[...]

    x = x_ref[...]
    m = mod_ref[...]
    h = _modulated_rmsnorm(x, g_ref[...], m[:, 0:1, :], m[:, 1:2, :])
    p = jnp.dot(h.reshape(rows, -1).astype(BF16), win_ref[...], preferred_element_type=F32)
    bg = p[:, :d_c]
    z = p[:, d_c:2 * d_c] * p[:, 2 * d_c:]
    ext_ref[:, HIST_C:HIST_C + tt, :] = z.reshape(nb, tt, d_c)
    conv = dwc_ref[0:1, :] * ext_ref[:, first:first + tt, :]
    for k in range(1, n_tap):
        conv = conv + dwc_ref[k:k + 1, :] * ext_ref[:, first + k:first + k + tt, :]
    gated = (bg * conv.reshape(rows, d_c)).astype(BF16)
    y = jnp.dot(gated, wout_ref[...], preferred_element_type=F32)
    xo_ref[...] = x + m[:, 2:3, :] * y.reshape(nb, tt, -1)

    @pl.when(t == n_tblk - 1)
    def _():
        nc_ref[...] = ext_ref[:, HIST_C + tt - (n_tap - 1):HIST_C + tt, :]

    ext_ref[:, 0:HIST_C, :] = ext_ref[:, tt:tt + HIST_C, :]


def _mix_odd_call(x, mod, prev_c, g_mix, w_in, dw_c, w_out, *, tile_tokens):
    b, t, d = x.shape
    nb, tt = _token_tile(b, t, tile_tokens)
    n_tblk = t // tt
    d_c = prev_c.shape[-1]
    n_tap = dw_c.shape[0]
    assert n_tap - 1 <= HIST_C and tt >= HIST_C
    const2 = lambda i, j: (0, 0)
    kern = functools.partial(_mix_odd_kernel, nb=nb, tt=tt, n_tblk=n_tblk)
    vmem = (4 * nb * tt * d * 4 + 2 * (w_in.size + w_out.size) * 2
            + nb * (HIST_C + tt) * d_c * 4 + 12 * nb * tt * d * 4)
    return pl.pallas_call(
        kern,
        out_shape=(jax.ShapeDtypeStruct(x.shape, F32), jax.ShapeDtypeStruct(prev_c.shape, F32)),
        grid=(b // nb, n_tblk),
        in_specs=[
            pl.BlockSpec((nb, tt, d), lambda i, j: (i, j, 0)),
            pl.BlockSpec((nb, N_MOD, d), lambda i, j: (i, 0, 0)),
            pl.BlockSpec((nb, n_tap - 1, d_c), lambda i, j: (i, 0, 0)),
            pl.BlockSpec((1, d), const2),
            pl.BlockSpec(w_in.shape, const2),
            pl.BlockSpec(dw_c.shape, const2),
            pl.BlockSpec(w_out.shape, const2),
        ],
        out_specs=(
            pl.BlockSpec((nb, tt, d), lambda i, j: (i, j, 0)),
            pl.BlockSpec((nb, n_tap - 1, d_c), lambda i, j: (i, 0, 0)),
        ),
        scratch_shapes=[pltpu.VMEM((nb, HIST_C + tt, d_c), F32)],
        compiler_params=pltpu.CompilerParams(
            dimension_semantics=("arbitrary", "arbitrary"),
            vmem_limit_bytes=_vmem_limit(vmem),
        ),
        name="mix_odd",
    )(x, mod, prev_c, g_mix.reshape(1, d), w_in, dw_c, w_out)


def _hi(a, b):
    if a is None:
        return b
    if b is None:
        return a
    return jnp.maximum(a, b)


def _lo(a, b):
    if a is None or b is None:
        return None
    return jnp.minimum(a, b)


def _bitonic_merge_desc(xs):
    n = len(xs)
    xs = list(xs)
    j = n // 2
    while j >= 1:
        for i in range(n):
            l = i ^ j
            if l > i:
                xs[i], xs[l] = _hi(xs[i], xs[l]), _lo(xs[i], xs[l])
        j //= 2
    return xs


def _bitonic_sort_desc(xs):
    n = len(xs)
    xs = list(xs)
    k = 2
    while k <= n:
        j = k // 2
        while j >= 1:
            for i in range(n):
                l = i ^ j
                if l > i:
                    hi, lo = _hi(xs[i], xs[l]), _lo(xs[i], xs[l])
                    if (i & k) == 0 or k == n:
                        xs[i], xs[l] = hi, lo
                    else:
                        xs[i], xs[l] = lo, hi
            j //= 2
        k *= 2
    return xs


def _merge_top(a, b):
    n = len(a)
    return _bitonic_merge_desc([_hi(a[k], b[n - 1 - k]) for k in range(n)])


def _top_sorted(xs, k):
    xs = list(xs) + [None] * ((-len(xs)) % k)
    groups = [_bitonic_sort_desc(xs[i:i + k]) for i in range(0, len(xs), k)]
    while len(groups) > 1:
        nxt = [_merge_top(groups[i], groups[i + 1]) for i in range(0, len(groups) - 1, 2)]
        if len(groups) % 2:
            nxt.append(groups[-1])
        groups = nxt
    return groups[0]


def _top_over_sublanes(vregs, k):
    xs = _bitonic_sort_desc(vregs)
    shift = SUBLANES // 2
    while shift >= 1:
        rolled = [pltpu.roll(x, shift, 0) for x in xs]
        xs = _merge_top(xs, rolled)
        shift //= 2
    return xs


def _dup_bf16_words(x):
    u = pltpu.bitcast(x, jnp.int32)
    u = u + (0x7FFF + (lax.shift_right_logical(u, 16) & 1))
    hi = u & jnp.int32(-65536)
    return pltpu.bitcast(hi | lax.shift_right_logical(hi, 16), F32)


def _peer_group_factors(load_s1, load_s2, n_heads, n_vreg):
    sub = lax.broadcasted_iota(jnp.int32, (SUBLANES, LANES), 0)
    pairs = [(a, b) for a in range(PEER_TOPK) for b in range(PEER_TOPK)
             if (a + 1) * (b + 1) <= PEER_TOPK]
    v1 = [jnp.zeros((SUBLANES, LANES), F32)] * PEER_TOPK
    v2 = [jnp.zeros((SUBLANES, LANES), F32)] * PEER_TOPK
    for hd in range(n_heads):
        t1 = _top_over_sublanes([load_s1(hd, v) for v in range(n_vreg)], PEER_TOPK)
        t2 = _top_over_sublanes([load_s2(hd, v) for v in range(n_vreg)], PEER_TOPK)
        v1 = [jnp.where(sub == hd, t1[a], v1[a]) for a in range(PEER_TOPK)]
        v2 = [jnp.where(sub == hd, t2[a], v2[a]) for a in range(PEER_TOPK)]
    top = _top_sorted([v1[a] + v2[b] for a, b in pairs], PEER_TOPK)
    tau = top[PEER_TOPK - 1]
    z = jnp.ones((SUBLANES, LANES), F32)
    for k in range(1, PEER_TOPK):
        z = z + jnp.exp(top[k] - top[0])
    inv_z = 1.0 / z
    for hd in range(n_heads):
        row = lambda arr: jnp.broadcast_to(arr[hd:hd + 1, :], (SUBLANES, LANES))
        m1, m2, iz, tau_h = row(v1[0]), row(v2[0]), row(inv_z), row(tau)
        t2 = [row(v2[b]) for b in range(PEER_TOPK)]
        for vp in range(n_vreg // 2):
            e1s, cnts, e2s, r2s = [], [], [], []
            for v in (2 * vp, 2 * vp + 1):
                s1v = load_s1(hd, v)
                s2v = load_s2(hd, v)
                cnt = jnp.zeros((SUBLANES, LANES), F32)
                for b in range(PEER_TOPK):
                    cnt = jnp.where(s1v + t2[b] >= tau_h, float(b + 1), cnt)
                r2 = jnp.full((SUBLANES, LANES), float(PEER_TOPK), F32)
                for b in reversed(range(PEER_TOPK)):
                    r2 = jnp.where(s2v >= t2[b], float(b), r2)
                e1s.append(_dup_bf16_words(jnp.exp(s1v - m1) * iz))
                cnts.append(_dup_bf16_words(cnt))
                e2s.append(jnp.exp(s2v - m2))
                r2s.append(r2)
            yield (hd, vp, jnp.concatenate(e1s, axis=0), jnp.concatenate(cnts, axis=0),
                   jnp.concatenate(e2s, axis=0), jnp.concatenate(r2s, axis=0))


def _peer_prep_kernel(x_ref, mod_ref, g_ref, wqt_ref, k1_ref, k2_ref,
                      ht_ref, e1n_ref, cnt_ref, e2_ref, r2_ref, s1_ref, s2_ref, *, nb, tt):
    rows = nb * tt
    n_heads, n_keys, d_half = k1_ref.shape
    x = x_ref[...]
    m = mod_ref[...]
    h = _modulated_rmsnorm(x, g_ref[...], m[:, 3:4, :], m[:, 4:5, :]).reshape(rows, -1)
    ht = h.T.astype(BF16)
    n_lg = rows // LANES
    for lg in range(n_lg):
        ht_ref[lg] = ht[:, lg * LANES:(lg + 1) * LANES]
    qt = jnp.dot(wqt_ref[...], ht, preferred_element_type=F32).astype(BF16)
    for hd in range(n_heads):
        base = hd * 2 * d_half
        sc1 = jnp.dot(k1_ref[hd], qt[base:base + d_half], preferred_element_type=F32)
        sc2 = jnp.dot(k2_ref[hd], qt[base + d_half:base + 2 * d_half], preferred_element_type=F32)
        for lg in range(n_lg):
            s1_ref[hd, lg] = sc1[:, lg * LANES:(lg + 1) * LANES]
            s2_ref[hd, lg] = sc2[:, lg * LANES:(lg + 1) * LANES]

    n_vreg = n_keys // SUBLANES
    assert n_vreg == PEER_TOPK and n_heads == SUBLANES

    def lane_group(lg, carry):
        load_s1 = lambda hd, v: s1_ref[hd, lg, v * SUBLANES:(v + 1) * SUBLANES, :]
        load_s2 = lambda hd, v: s2_ref[hd, lg, v * SUBLANES:(v + 1) * SUBLANES, :]
        for hd, vp, e1d, cntd, e2, r2 in _peer_group_factors(load_s1, load_s2, n_heads, n_vreg):
            rs = slice(vp * 2 * SUBLANES, (vp + 1) * 2 * SUBLANES)
            e1n_ref[hd, lg, rs, :] = e1d
            cnt_ref[hd, lg, rs, :] = cntd
            ws = slice(vp * SUBLANES, (vp + 1) * SUBLANES)
            e2_ref[hd, lg, ws, :] = pltpu.bitcast(e2.astype(BF16), F32)
            r2_ref[hd, lg, ws, :] = pltpu.bitcast(r2.astype(BF16), F32)
        return carry

    lax.fori_loop(0, n_lg, lane_group, 0)


def _peer_prep_call(x, mod, g_ffn, wqt, k1, k2, *, tile_tokens):
    b, t, d = x.shape
    nb, tt = _token_tile(b, t, tile_tokens)
    rows = nb * tt
    n_tb = t // tt
    ntok = b * t
    n_heads, n_keys, _ = k1.shape
    kern = functools.partial(_peer_prep_kernel, nb=nb, tt=tt)
    blk = lambda i: (i // n_tb, i % n_tb, 0)
    n_lg = rows // LANES
    n_grp = ntok // LANES
    f_shape = jax.ShapeDtypeStruct((n_heads, n_grp, n_keys, LANES), F32)
    b_shape = jax.ShapeDtypeStruct((n_heads, n_grp, n_keys // 2, LANES), F32)
    e_spec = pl.BlockSpec((n_heads, n_lg, n_keys, LANES), lambda i: (0, i, 0, 0))
    b_spec = pl.BlockSpec((n_heads, n_lg, n_keys // 2, LANES), lambda i: (0, i, 0, 0))
    vmem = (2 * rows * d * 4 + 2 * wqt.size * 2 + 2 * rows * d * 2 + 8 * n_heads * n_keys * rows * 4
            + 6 * rows * d * 4 + wqt.shape[0] * rows * 6)
    return pl.pallas_call(
        kern,
        out_shape=(jax.ShapeDtypeStruct((n_grp, d, LANES), BF16), f_shape, f_shape, b_shape, b_shape),
        grid=(ntok // rows,),
        in_specs=[
            pl.BlockSpec((nb, tt, d), blk),
            pl.BlockSpec((nb, N_MOD, d), lambda i: (i // n_tb, 0, 0)),
            pl.BlockSpec((1, d), lambda i: (0, 0)),
            pl.BlockSpec(wqt.shape, lambda i: (0, 0)),
            pl.BlockSpec(k1.shape, lambda i: (0, 0, 0)),
            pl.BlockSpec(k2.shape, lambda i: (0, 0, 0)),
        ],
        out_specs=(pl.BlockSpec((n_lg, d, LANES), lambda i: (i, 0, 0)), e_spec, e_spec, b_spec, b_spec),
        scratch_shapes=[pltpu.VMEM((n_heads, n_lg, n_keys, LANES), F32),
                        pltpu.VMEM((n_heads, n_lg, n_keys, LANES), F32)],
        compiler_params=pltpu.CompilerParams(
            dimension_semantics=("arbitrary",),
            vmem_limit_bytes=_vmem_limit(vmem),
        ),
        name="peer_prep",
    )(x, mod, g_ffn.reshape(1, d), wqt, k1, k2)


def _peer_main_kernel(x_ref, mod_ref, ht_ref, e1n_ref, cnt_ref, e2_ref, r2_ref, u_ref, vt_ref, fg_ref,
                      o_ref, act0_ref, act1_ref, prod0_ref, prod1_ref, acc_ref,
                      *, nb, tt, n_chunks, total, final_norm):
    s = pl.program_id(0)
    n_heads = e2_ref.shape[0]
    n_keys = 2 * e2_ref.shape[2]
    rows = nb * tt
    e_c = u_ref.shape[0]
    d = acc_ref.shape[1]
    s3 = jnp.clip(s - 2, 0, total - 1)
    c3 = s3 % n_chunks

    @pl.when(s == 0)
    def _():
        act1_ref[...] = jnp.zeros(act1_ref.shape, F32)
        prod0_ref[...] = jnp.zeros(prod0_ref.shape, BF16)
        prod1_ref[...] = jnp.zeros(prod1_ref.shape, BF16)

    @pl.when(c3 == 0)
    def _():
        acc_ref[...] = jnp.zeros(acc_ref.shape, F32)

    def step(act_w, act_r, prod_w, prod_r):
        def tile(ii, lg):
            er = slice(ii * n_keys, (ii + 1) * n_keys)
            row_tile = lambda ref, hd: pltpu.bitcast(
                jnp.broadcast_to(ref[hd, lg, ii:ii + 1, :], (n_keys // 2, LANES)), BF16)
            w = None
            for hd in range(n_heads):
                e2t = pltpu.bitcast(e2_ref[hd, lg], BF16)
                r2t = pltpu.bitcast(r2_ref[hd, lg], BF16)
                term = jnp.where(r2t < row_tile(cnt_ref, hd), e2t, 0.0)
                term = term * row_tile(e1n_ref, hd)
                w = term if w is None else w + term
            a = act_r[lg, er, :]
            prod_w[lg, er, :] = _gelu_tanh(a).astype(BF16) * w

        n_lg = rows // LANES
        assert n_lg % 2 == 0
        tiles = [(ii, lg) for ii in range(e_c // n_keys) for lg in range(n_lg)]
        pieces = []
        for r in range(0, e_c, MXU_PIECE_ROWS):
            pieces += [(1, slice(r, r + MXU_PIECE_ROWS), g) for g in range(0, n_lg, 2)]
        for r in range(0, d, MXU_PIECE_ROWS):
            pieces += [(3, slice(r, r + MXU_PIECE_ROWS), g) for g in range(0, n_lg, 2)]
        per_piece = len(tiles) // len(pieces)
        assert per_piece * len(pieces) == len(tiles)
        for k, (stage, rs, g) in enumerate(pieces):
            if stage == 1:
                rhs = jnp.concatenate([ht_ref[g], ht_ref[g + 1]], axis=1)
                res = jnp.dot(u_ref[rs, :], rhs, preferred_element_type=F32)
                act_w[g, rs, :] = res[:, :LANES]
                act_w[g + 1, rs, :] = res[:, LANES:]
            else:
                rhs = jnp.concatenate([prod_r[g], prod_r[g + 1]], axis=1)
                res = jnp.dot(vt_ref[rs, :], rhs, preferred_element_type=F32)
                acc_ref[g, rs, :] += res[:, :LANES]
                acc_ref[g + 1, rs, :] += res[:, LANES:]
            for ii, lg in tiles[k * per_piece:(k + 1) * per_piece]:
                tile(ii, lg)

    @pl.when(s % 2 == 0)
    def _():
        step(act0_ref, act1_ref, prod1_ref, prod0_ref)

    @pl.when(s % 2 == 1)
    def _():
        step(act1_ref, act0_ref, prod0_ref, prod1_ref)

    @pl.when(jnp.logical_and(c3 == n_chunks - 1, s >= 2))
    def _():
        m = mod_ref[...]
        out = jnp.concatenate([acc_ref[lg].T for lg in range(rows // LANES)], axis=0)
        y = x_ref[...] + m[:, 5:6, :] * out.reshape(nb, tt, -1)
        if final_norm:
            y = (y * lax.rsqrt(jnp.mean(y * y, axis=-1, keepdims=True) + EPS)) * fg_ref[...]
        o_ref[...] = y


def _peer_main_call(x, mod, ht, e1n, cnt, e2, r2, u_bf, vt_bf, final_g, *, tile_tokens, chunk_experts,
                    final_norm):
    b, t, d = x.shape
    nb, tt = _token_tile(b, t, tile_tokens)
    rows = nb * tt
    n_tb = t // tt
    ntok = b * t
    n_blocks = ntok // rows
    n_exp = u_bf.shape[0]
    n_heads, _, n_keys, _ = e1n.shape
    n_lg = rows // LANES
    e_c = chunk_experts
    n_chunks = n_exp // e_c
    n_i = e_c // n_keys
    assert n_i % SUBLANES == 0 and n_chunks > 1
    total = n_blocks * n_chunks
    last = total - 1
    st1 = lambda s: jnp.minimum(s, last)
    st2 = lambda s: jnp.clip(s - 1, 0, last)
    st3 = lambda s: jnp.clip(s - 2, 0, last)
    xblk = lambda s: ((st3(s) // n_chunks) // n_tb, (st3(s) // n_chunks) % n_tb, 0)
    kern = functools.partial(_peer_main_kernel, nb=nb, tt=tt, n_chunks=n_chunks, total=total,
                             final_norm=final_norm)
    vmem = (4 * rows * d * 4 + 2 * d * rows * 2 + 2 * n_heads * n_keys * rows * 4
            + 4 * n_heads * n_i * rows * 4 + 4 * e_c * d * 2 + 2 * e_c * rows * 4
            + 2 * e_c * rows * 2 + d * rows * 4 + 3 * rows * d * 4)
    return pl.pallas_call(
        kern,
        out_shape=jax.ShapeDtypeStruct(x.shape, F32),
        grid=(total + 2,),
        in_specs=[
            pl.BlockSpec((nb, tt, d), xblk),
            pl.BlockSpec((nb, N_MOD, d), lambda s: ((st3(s) // n_chunks) // n_tb, 0, 0)),
            pl.BlockSpec((n_lg, d, LANES), lambda s: (st1(s) // n_chunks, 0, 0)),
            pl.BlockSpec((n_heads, n_lg, n_i, LANES),
                         lambda s: (0, st2(s) // n_chunks, st2(s) % n_chunks, 0)),
            pl.BlockSpec((n_heads, n_lg, n_i, LANES),
                         lambda s: (0, st2(s) // n_chunks, st2(s) % n_chunks, 0)),
            pl.BlockSpec((n_heads, n_lg, n_keys // 2, LANES), lambda s: (0, st2(s) // n_chunks, 0, 0)),
            pl.BlockSpec((n_heads, n_lg, n_keys // 2, LANES), lambda s: (0, st2(s) // n_chunks, 0, 0)),
            pl.BlockSpec((e_c, d), lambda s: (st1(s) % n_chunks, 0)),
            pl.BlockSpec((d, e_c), lambda s: (0, st3(s) % n_chunks)),
            pl.BlockSpec((1, d), lambda s: (0, 0)),
        ],
        out_specs=pl.BlockSpec((nb, tt, d), xblk),
        scratch_shapes=[pltpu.VMEM((n_lg, e_c, LANES), F32), pltpu.VMEM((n_lg, e_c, LANES), F32),
                        pltpu.VMEM((n_lg, e_c, LANES), BF16), pltpu.VMEM((n_lg, e_c, LANES), BF16),
                        pltpu.VMEM((n_lg, d, LANES), F32)],
        compiler_params=pltpu.CompilerParams(
            dimension_semantics=("arbitrary",),
            vmem_limit_bytes=_vmem_limit(vmem),
        ),
        name="peer_main",
    )(x, mod, ht, e1n, cnt, e2, r2, u_bf, vt_bf, final_g.reshape(1, d))


MIX_TILE_TOKENS = 256
PREP_TILE_TOKENS = 256
MAIN_TILE_TOKENS = 512
MAIN_CHUNK_EXPERTS = 1024


def _run_trunk(x, mod, prev_a, prev_b, prev_c, pos0, wts):
    depth = mod.shape[0]
    new_a, new_b, new_c = [], [], []
    for l in range(depth):
        i = l // 2
        if l % 2 == 0:
            x, sa, sb = _mix_even_call(
                x, mod[l], prev_a[i], prev_b[i], wts["norm_mix_g"][l], wts["w_in_even"][i],
                wts["dw_a"][i], wts["db_a"][i], wts["ln_a_g"][i], wts["ln_a_b"][i], wts["w_pool"][i],
                wts["pool_scale"][i], wts["w_out_even"][i], pos0=pos0, tile_tokens=MIX_TILE_TOKENS)
            new_a.append(sa)
            new_b.append(sb)
        else:
            x, sc = _mix_odd_call(x, mod[l], prev_c[i], wts["norm_mix_g"][l], wts["w_in_odd"][i],
                                  wts["dw_c"][i], wts["w_out_odd"][i], tile_tokens=MIX_TILE_TOKENS)
            new_c.append(sc)
        ht, e1n, cnt, e2, r2 = _peer_prep_call(x, mod[l], wts["norm_ffn_g"][l], wts["peer_wqt"][l],
                                           wts["peer_k1"][l], wts["peer_k2"][l],
                                           tile_tokens=PREP_TILE_TOKENS)
        x = _peer_main_call(x, mod[l], ht, e1n, cnt, e2, r2, wts["peer_u"][l], wts["peer_vt"][l],
                            wts["final_g"], tile_tokens=MAIN_TILE_TOKENS,
                            chunk_experts=MAIN_CHUNK_EXPERTS, final_norm=(l == depth - 1))
    return x, jnp.stack(new_a), jnp.stack(new_b), jnp.stack(new_c)


def kernel(x_prompt, x_sample, state_conv_a, state_pool_b, state_conv_c, c_prompt, c_sample, norm_mix_g, norm_ffn_g, w_ada, b_ada, w_in_even, dw_a, db_a, ln_a_g, ln_a_b, w_pool, pool_scale, w_out_even, w_in_odd, dw_c, w_out_odd, peer_wq, peer_k1, peer_k2, peer_u, peer_v, final_g):
    bp, _, d = x_prompt.shape
    bs = x_sample.shape[0]
    depth = w_ada.shape[0]
    wts = dict(
        norm_mix_g=norm_mix_g, norm_ffn_g=norm_ffn_g, final_g=final_g,
        w_in_even=w_in_even.astype(BF16), dw_a=dw_a, db_a=db_a, ln_a_g=ln_a_g, ln_a_b=ln_a_b,
        w_pool=w_pool.astype(BF16), pool_scale=pool_scale, w_out_even=w_out_even.astype(BF16),
        w_in_odd=w_in_odd.astype(BF16), dw_c=dw_c, w_out_odd=w_out_odd.astype(BF16),
        peer_wqt=jnp.swapaxes(peer_wq, 1, 2).astype(BF16),
        peer_k1=peer_k1.astype(BF16), peer_k2=peer_k2.astype(BF16),
        peer_u=peer_u.astype(BF16), peer_vt=jnp.swapaxes(peer_v, 1, 2).astype(BF16),
    )
    mod = _ada_call(jnp.concatenate([c_prompt, c_sample], axis=0), w_ada, b_ada)
    mod = mod.reshape(depth, bp + bs, N_MOD, d)
    n_even, _, ha, d_a = state_conv_a.shape
    _, _, hb, d_b = state_pool_b.shape
    n_odd, _, hc, d_c = state_conv_c.shape
    y_p, pa_a, pa_b, pa_c = _run_trunk(
        x_prompt, mod[:, :bp], jnp.zeros((n_even, bp, ha, d_a), F32),
        jnp.zeros((n_even, bp, hb, d_b), F32), jnp.zeros((n_odd, bp, hc, d_c), F32), 0, wts)
    y_s, sa_a, sa_b, sa_c = _run_trunk(
        x_sample, mod[:, bp:], state_conv_a, state_pool_b, state_conv_c, PAST_LEN, wts)
    return (y_p, y_s, pa_a, pa_b, pa_c, sa_a, sa_b, sa_c)
```

```python
import functools

import jax
import jax.numpy as jnp
from jax import lax
from jax.experimental import pallas as pl
from jax.experimental.pallas import tpu as pltpu

F32 = jnp.float32
BF16 = jnp.bfloat16

PAST_LEN = 4096
POOL_WINDOWS = (2, 4, 8, 16)
PEER_TOPK = 16
N_MOD = 6
EPS = 1e-6

LANES = 128
SUBLANES = 8
VMEM_LIMIT_CAP = 56 * 1024 * 1024

HIST_A = 32
HIST_B = 16
HIST_C = 8
CONV_ROW_CHUNK = 32
MXU_PIECE_ROWS = 256


def _vmem_limit(nbytes):
    return int(min(VMEM_LIMIT_CAP, max(16 * 1024 * 1024, nbytes)))


def _token_tile(batch, seq, target):
    if seq >= target:
        assert seq % target == 0
        return 1, target
    nb = min(batch, target // seq)
    assert batch % nb == 0
    return nb, seq


def _ada_kernel(c_ref, w_ref, b_ref, o_ref):
    c = c_ref[...]
    ca = (c * jax.nn.sigmoid(c)).astype(BF16)
    o_ref[0] = jnp.dot(ca, w_ref[0].astype(BF16), preferred_element_type=F32) + b_ref[0]


def _ada_call(c_all, w_ada, b_ada):
    depth, d, n = w_ada.shape
    bt = c_all.shape[0]
    tn = 1024
    return pl.pallas_call(
        _ada_kernel,
        out_shape=jax.ShapeDtypeStruct((depth, bt, n), F32),
        grid=(depth, n // tn),
        in_specs=[
            pl.BlockSpec((bt, d), lambda l, j: (0, 0)),
            pl.BlockSpec((1, d, tn), lambda l, j: (l, 0, j)),
            pl.BlockSpec((1, 1, tn), lambda l, j: (l, 0, j)),
        ],
        out_specs=pl.BlockSpec((1, bt, tn), lambda l, j: (l, 0, j)),
        compiler_params=pltpu.CompilerParams(
            dimension_semantics=("arbitrary", "arbitrary"),
            vmem_limit_bytes=_vmem_limit(4 * d * tn * 4),
        ),
        name="ada_mod",
    )(c_all, w_ada, b_ada.reshape(depth, 1, n))


def _modulated_rmsnorm(x, g, shift, scale):
    y = x * lax.rsqrt(jnp.mean(x * x, axis=-1, keepdims=True) + EPS)
    return (y * g) * (1.0 + scale) + shift


def _gelu_tanh(x):
    c0 = 0.7978845608028654
    inner = x * (c0 + (c0 * 0.044715) * (x * x))
    return (0.5 * x) * (1.0 + jnp.tanh(inner))


def _mix_even_kernel(x_ref, mod_ref, pa_ref, pb_ref, g_ref, win_ref, dwa_ref, dba_ref, lng_ref,
                     lnb_ref, wpool_ref, pscale_ref, wout_ref,
                     xo_ref, na_ref, nbo_ref, exta_ref, extb_ref, *, nb, tt, pos0, n_tblk):
    t = pl.program_id(1)
    d_a = exta_ref.shape[-1]
    d_b = extb_ref.shape[-1]
    n_tap = dwa_ref.shape[0]
    rows = nb * tt

    @pl.when(t == 0)
    def _():
        exta_ref[:, 0:HIST_A - (n_tap - 1), :] = jnp.zeros((nb, HIST_A - (n_tap - 1), d_a), F32)
        exta_ref[:, HIST_A - (n_tap - 1):HIST_A, :] = pa_ref[...]
        extb_ref[:, 0:1, :] = jnp.zeros((nb, 1, d_b), F32)
        extb_ref[:, 1:HIST_B, :] = pb_ref[...]

    x = x_ref[...]
    m = mod_ref[...]
    h = _modulated_rmsnorm(x, g_ref[...], m[:, 0:1, :], m[:, 1:2, :])
    p = jnp.dot(h.reshape(rows, -1).astype(BF16), win_ref[...], preferred_element_type=F32)
    a = p[:, :d_a] * jax.nn.sigmoid(p[:, d_a:2 * d_a])
    b_in = p[:, 2 * d_a:]
    exta_ref[:, HIST_A:HIST_A + tt, :] = a.reshape(nb, tt, d_a)
    extb_ref[:, HIST_B:HIST_B + tt, :] = b_in.reshape(nb, tt, d_b)

    taps = [dwa_ref[k:k + 1, :] for k in range(n_tap)]
    first = HIST_A - (n_tap - 1)
    pieces = []
    for bi in range(nb):
        for r in range(0, tt, CONV_ROW_CHUNK):
            rc = min(CONV_ROW_CHUNK, tt - r)
            acc = taps[0] * exta_ref[bi, first + r:first + r + rc, :]
            for k in range(1, n_tap):
                acc = acc + taps[k] * exta_ref[bi, first + r + k:first + r + k + rc, :]
            pieces.append(acc)
    conv = jnp.concatenate(pieces, axis=0) + dba_ref[...]
    mu = jnp.mean(conv, axis=-1, keepdims=True)
    cen = conv - mu
    var = jnp.mean(cen * cen, axis=-1, keepdims=True)
    ln = (cen * lax.rsqrt(var + EPS)) * lng_ref[...] + lnb_ref[...]
    a_out = ln * jax.nn.sigmoid(ln)

    dg = d_b // len(POOL_WINDOWS)
    pos = pos0 + t * tt + lax.broadcasted_iota(jnp.int32, (nb, tt, dg), 1)
    b_groups = []
    for gi, w in enumerate(POOL_WINDOWS):
        sl = slice(gi * dg, (gi + 1) * dg)
        s = extb_ref[:, HIST_B:HIST_B + tt, sl]
        for k in range(1, w):
            s = s + extb_ref[:, HIST_B - k:HIST_B - k + tt, sl]
        cnt = jnp.minimum(w, pos + 1).astype(F32)
        pooled = s / cnt - extb_ref[:, HIST_B:HIST_B + tt, sl]
        b_groups.append(jnp.dot(pooled.reshape(rows, dg).astype(BF16), wpool_ref[gi],
                                preferred_element_type=F32))
    b_out = jnp.concatenate(b_groups, axis=-1) * pscale_ref[...]

    ab = jnp.concatenate([a_out, b_out], axis=-1).astype(BF16)
    y = jnp.dot(ab, wout_ref[...], preferred_element_type=F32)
    xo_ref[...] = x + m[:, 2:3, :] * y.reshape(nb, tt, -1)

    @pl.when(t == n_tblk - 1)
    def _():
        na_ref[...] = exta_ref[:, HIST_A + tt - (n_tap - 1):HIST_A + tt, :]
        nbo_ref[...] = extb_ref[:, HIST_B + tt - (HIST_B - 1):HIST_B + tt, :]

    exta_ref[:, 0:HIST_A, :] = exta_ref[:, tt:tt + HIST_A, :]
    extb_ref[:, 0:HIST_B, :] = extb_ref[:, tt:tt + HIST_B, :]


def _mix_even_call(x, mod, prev_a, prev_b, g_mix, w_in, dw_a, db_a, ln_g, ln_b, w_pool, pool_scale,
                   w_out, *, pos0, tile_tokens):
    b, t, d = x.shape
    nb, tt = _token_tile(b, t, tile_tokens)
    n_tblk = t // tt
    d_a = prev_a.shape[-1]
    d_b = prev_b.shape[-1]
    n_tap = dw_a.shape[0]
    assert n_tap - 1 <= HIST_A and prev_b.shape[1] == HIST_B - 1 and tt >= HIST_A
    const2 = lambda i, j: (0, 0)
    kern = functools.partial(_mix_even_kernel, nb=nb, tt=tt, pos0=pos0, n_tblk=n_tblk)
    vmem = (4 * nb * tt * d * 4 + 2 * (w_in.size + w_out.size) * 2
            + nb * (HIST_A + HIST_B + 2 * tt) * d_a * 4 + 12 * nb * tt * d * 4)
    return pl.pallas_call(
        kern,
        out_shape=(jax.ShapeDtypeStruct(x.shape, F32),
                   jax.ShapeDtypeStruct(prev_a.shape, F32),
                   jax.ShapeDtypeStruct(prev_b.shape, F32)),
        grid=(b // nb, n_tblk),
        in_specs=[
            pl.BlockSpec((nb, tt, d), lambda i, j: (i, j, 0)),
            pl.BlockSpec((nb, N_MOD, d), lambda i, j: (i, 0, 0)),
            pl.BlockSpec((nb, n_tap - 1, d_a), lambda i, j: (i, 0, 0)),
            pl.BlockSpec((nb, HIST_B - 1, d_b), lambda i, j: (i, 0, 0)),
            pl.BlockSpec((1, d), const2),
            pl.BlockSpec(w_in.shape, const2),
            pl.BlockSpec(dw_a.shape, const2),
            pl.BlockSpec((1, d_a), const2),
            pl.BlockSpec((1, d_a), const2),
            pl.BlockSpec((1, d_a), const2),
            pl.BlockSpec(w_pool.shape, lambda i, j: (0, 0, 0)),
            pl.BlockSpec((1, d_b), const2),
            pl.BlockSpec(w_out.shape, const2),
        ],
        out_specs=(
            pl.BlockSpec((nb, tt, d), lambda i, j: (i, j, 0)),
            pl.BlockSpec((nb, n_tap - 1, d_a), lambda i, j: (i, 0, 0)),
            pl.BlockSpec((nb, HIST_B - 1, d_b), lambda i, j: (i, 0, 0)),
        ),
        scratch_shapes=[pltpu.VMEM((nb, HIST_A + tt, d_a), F32),
                        pltpu.VMEM((nb, HIST_B + tt, d_b), F32)],
        compiler_params=pltpu.CompilerParams(
            dimension_semantics=("arbitrary", "arbitrary"),
            vmem_limit_bytes=_vmem_limit(vmem),
        ),
        name="mix_even",
    )(x, mod, prev_a, prev_b, g_mix.reshape(1, d), w_in, dw_a, db_a.reshape(1, d_a),
      ln_g.reshape(1, d_a), ln_b.reshape(1, d_a), w_pool, pool_scale.reshape(1, d_b), w_out)


def _mix_odd_kernel(x_ref, mod_ref, pc_ref, g_ref, win_ref, dwc_ref, wout_ref,
                    xo_ref, nc_ref, ext_ref, *, nb, tt, n_tblk):
    t = pl.program_id(1)
    d_c = ext_ref.shape[-1]
    n_tap = dwc_ref.shape[0]
    rows = nb * tt
    first = HIST_C - (n_tap - 1)

    @pl.when(t == 0)
    def _():
        ext_ref[:, 0:first, :] = jnp.zeros((nb, first, d_c), F32)
        ext_ref[:, first:HIST_C, :] = pc_ref[...]

    x = x_ref[...]
    m = mod_ref[...]
    h = _modulated_rmsnorm(x, g_ref[...], m[:, 0:1, :], m[:, 1:2, :])
    p = jnp.dot(h.reshape(rows, -1).astype(BF16), win_ref[...], preferred_element_type=F32)
    bg = p[:, :d_c]
    z = p[:, d_c:2 * d_c] * p[:, 2 * d_c:]
    ext_ref[:, HIST_C:HIST_C + tt, :] = z.reshape(nb, tt, d_c)
    conv = dwc_ref[0:1, :] * ext_ref[:, first:first + tt, :]
    for k in range(1, n_tap):
        conv = conv + dwc_ref[k:k + 1, :] * ext_ref[:, first + k:first + k + tt, :]
    gated = (bg * conv.reshape(rows, d_c)).astype(BF16)
    y = jnp.dot(gated, wout_ref[...], preferred_element_type=F32)
    xo_ref[...] = x + m[:, 2:3, :] * y.reshape(nb, tt, -1)

    @pl.when(t == n_tblk - 1)
    def _():
        nc_ref[...] = ext_ref[:, HIST_C + tt - (n_tap - 1):HIST_C + tt, :]

    ext_ref[:, 0:HIST_C, :] = ext_ref[:, tt:tt + HIST_C, :]


def _mix_odd_call(x, mod, prev_c, g_mix, w_in, dw_c, w_out, *, tile_tokens):
    b, t, d = x.shape
    nb, tt = _token_tile(b, t, tile_tokens)
    n_tblk = t // tt
    d_c = prev_c.shape[-1]
    n_tap = dw_c.shape[0]
    assert n_tap - 1 <= HIST_C and tt >= HIST_C
    const2 = lambda i, j: (0, 0)
    kern = functools.partial(_mix_odd_kernel, nb=nb, tt=tt, n_tblk=n_tblk)
    vmem = (4 * nb * tt * d * 4 + 2 * (w_in.size + w_out.size) * 2
            + nb * (HIST_C + tt) * d_c * 4 + 12 * nb * tt * d * 4)
    return pl.pallas_call(
        kern,
        out_shape=(jax.ShapeDtypeStruct(x.shape, F32), jax.ShapeDtypeStruct(prev_c.shape, F32)),
        grid=(b // nb, n_tblk),
        in_specs=[
            pl.BlockSpec((nb, tt, d), lambda i, j: (i, j, 0)),
            pl.BlockSpec((nb, N_MOD, d), lambda i, j: (i, 0, 0)),
            pl.BlockSpec((nb, n_tap - 1, d_c), lambda i, j: (i, 0, 0)),
            pl.BlockSpec((1, d), const2),
            pl.BlockSpec(w_in.shape, const2),
            pl.BlockSpec(dw_c.shape, const2),
            pl.BlockSpec(w_out.shape, const2),
        ],
        out_specs=(
            pl.BlockSpec((nb, tt, d), lambda i, j: (i, j, 0)),
            pl.BlockSpec((nb, n_tap - 1, d_c), lambda i, j: (i, 0, 0)),
        ),
        scratch_shapes=[pltpu.VMEM((nb, HIST_C + tt, d_c), F32)],
        compiler_params=pltpu.CompilerParams(
            dimension_semantics=("arbitrary", "arbitrary"),
            vmem_limit_bytes=_vmem_limit(vmem),
        ),
        name="mix_odd",
    )(x, mod, prev_c, g_mix.reshape(1, d), w_in, dw_c, w_out)


def _hi(a, b):
    if a is None:
        return b
    if b is None:
        return a
    return jnp.maximum(a, b)


def _lo(a, b):
    if a is None or b is None:
        return None
    return jnp.minimum(a, b)


def _bitonic_merge_desc(xs):
    n = len(xs)
    xs = list(xs)
    j = n // 2
    while j >= 1:
        for i in range(n):
            l = i ^ j
            if l > i:
                xs[i], xs[l] = _hi(xs[i], xs[l]), _lo(xs[i], xs[l])
        j //= 2
    return xs


def _bitonic_sort_desc(xs):
    n = len(xs)
    xs = list(xs)
    k = 2
    while k <= n:
        j = k // 2
        while j >= 1:
            for i in range(n):
                l = i ^ j
                if l > i:
                    hi, lo = _hi(xs[i], xs[l]), _lo(xs[i], xs[l])
                    if (i & k) == 0 or k == n:
                        xs[i], xs[l] = hi, lo
                    else:
                        xs[i], xs[l] = lo, hi
            j //= 2
        k *= 2
    return xs


def _merge_top(a, b):
    n = len(a)
    return _bitonic_merge_desc([_hi(a[k], b[n - 1 - k]) for k in range(n)])


def _top_sorted(xs, k):
    xs = list(xs) + [None] * ((-len(xs)) % k)
    groups = [_bitonic_sort_desc(xs[i:i + k]) for i in range(0, len(xs), k)]
    while len(groups) > 1:
        nxt = [_merge_top(groups[i], groups[i + 1]) for i in range(0, len(groups) - 1, 2)]
        if len(groups) % 2:
            nxt.append(groups[-1])
        groups = nxt
    return groups[0]


def _top_over_sublanes(vregs, k):
    xs = _bitonic_sort_desc(vregs)
    shift = SUBLANES // 2
    while shift >= 1:
        rolled = [pltpu.roll(x, shift, 0) for x in xs]
        xs = _merge_top(xs, rolled)
        shift //= 2
    return xs


def _dup_bf16_words(x):
    u = pltpu.bitcast(x, jnp.int32)
    u = u + (0x7FFF + (lax.shift_right_logical(u, 16) & 1))
    hi = u & jnp.int32(-65536)
    return pltpu.bitcast(hi | lax.shift_right_logical(hi, 16), F32)


def _dup_count_constant(c):
    import numpy as np
    hi = np.asarray(float(c), np.float32).view(np.uint32) >> 16
    return float(((hi << 16) | hi).astype(np.uint32).view(np.float32))


_DUP_COUNT = [_dup_count_constant(c) for c in range(PEER_TOPK + 1)]


def _peer_group_factors(load_s1, load_s2, n_heads, n_vreg):
    sub = lax.broadcasted_iota(jnp.int32, (SUBLANES, LANES), 0)
    pairs = [(a, b) for a in range(PEER_TOPK) for b in range(PEER_TOPK)
             if (a + 1) * (b + 1) <= PEER_TOPK]
    v1 = [jnp.zeros((SUBLANES, LANES), F32)] * PEER_TOPK
    v2 = [jnp.zeros((SUBLANES, LANES), F32)] * PEER_TOPK
    for hd in range(n_heads):
        t1 = _top_over_sublanes([load_s1(hd, v) for v in range(n_vreg)], PEER_TOPK)
        t2 = _top_over_sublanes([load_s2(hd, v) for v in range(n_vreg)], PEER_TOPK)
        v1 = [jnp.where(sub == hd, t1[a], v1[a]) for a in range(PEER_TOPK)]
        v2 = [jnp.where(sub == hd, t2[a], v2[a]) for a in range(PEER_TOPK)]
    top = _top_sorted([v1[a] + v2[b] for a, b in pairs], PEER_TOPK)
    tau = top[PEER_TOPK - 1]
    z = jnp.ones((SUBLANES, LANES), F32)
    for k in range(1, PEER_TOPK):
        z = z + jnp.exp(top[k] - top[0])
    inv_z = 1.0 / z
    for hd in range(n_heads):
        row = lambda arr: jnp.broadcast_to(arr[hd:hd + 1, :], (SUBLANES, LANES))
        m1, m2, iz, tau_h = row(v1[0]), row(v2[0]), row(inv_z), row(tau)
        t2 = [row(v2[b]) for b in range(PEER_TOPK)]
        for vp in range(n_vreg // 2):
            e1s, cnts, e2s, r2s = [], [], [], []
            for v in (2 * vp, 2 * vp + 1):
                s1v = load_s1(hd, v)
                s2v = load_s2(hd, v)
                cnt = jnp.zeros((SUBLANES, LANES), F32)
                for b in range(PEER_TOPK):
                    cnt = jnp.where(s1v + t2[b] >= tau_h, _DUP_COUNT[b + 1], cnt)
                r2 = jnp.full((SUBLANES, LANES), float(PEER_TOPK), F32)
                for b in reversed(range(PEER_TOPK)):
                    r2 = jnp.where(s2v >= t2[b], float(b), r2)
                e1s.append(_dup_bf16_words(jnp.exp(s1v - m1) * iz))
                cnts.append(cnt)
                e2s.append(jnp.exp(s2v - m2))
                r2s.append(r2)
            yield (hd, vp, jnp.concatenate(e1s, axis=0), jnp.concatenate(cnts, axis=0),
                   jnp.concatenate(e2s, axis=0), jnp.concatenate(r2s, axis=0))


def _peer_prep_kernel(x_ref, mod_ref, g_ref, wqt_ref, k1_ref, k2_ref,
                      ht_ref, e1n_ref, cnt_ref, e2_ref, r2_ref, s1_ref, s2_ref, *, nb, tt):
    rows = nb * tt
    n_heads, n_keys, d_half = k1_ref.shape
    x = x_ref[...]
    m = mod_ref[...]
    h = _modulated_rmsnorm(x, g_ref[...], m[:, 3:4, :], m[:, 4:5, :]).reshape(rows, -1)
    ht = h.T.astype(BF16)
    n_lg = rows // LANES
    for lg in range(n_lg):
        ht_ref[lg] = ht[:, lg * LANES:(lg + 1) * LANES]
    qt = jnp.dot(wqt_ref[...], ht, preferred_element_type=F32).astype(BF16)
    for hd in range(n_heads):
        base = hd * 2 * d_half
        sc1 = jnp.dot(k1_ref[hd], qt[base:base + d_half], preferred_element_type=F32)
        sc2 = jnp.dot(k2_ref[hd], qt[base + d_half:base + 2 * d_half], preferred_element_type=F32)
        for lg in range(n_lg):
            s1_ref[hd, lg] = sc1[:, lg * LANES:(lg + 1) * LANES]
            s2_ref[hd, lg] = sc2[:, lg * LANES:(lg + 1) * LANES]

    n_vreg = n_keys // SUBLANES
    assert n_vreg == PEER_TOPK and n_heads == SUBLANES

    def lane_group(lg, carry):
        load_s1 = lambda hd, v: s1_ref[hd, lg, v * SUBLANES:(v + 1) * SUBLANES, :]
        load_s2 = lambda hd, v: s2_ref[hd, lg, v * SUBLANES:(v + 1) * SUBLANES, :]
        for hd, vp, e1d, cntd, e2, r2 in _peer_group_factors(load_s1, load_s2, n_heads, n_vreg):
            rs = slice(vp * 2 * SUBLANES, (vp + 1) * 2 * SUBLANES)
            e1n_ref[hd, lg, rs, :] = e1d
            cnt_ref[hd, lg, rs, :] = cntd
            ws = slice(vp * SUBLANES, (vp + 1) * SUBLANES)
            e2_ref[hd, lg, ws, :] = pltpu.bitcast(e2.astype(BF16), F32)
            r2_ref[hd, lg, ws, :] = pltpu.bitcast(r2.astype(BF16), F32)
        return carry

    lax.fori_loop(0, n_lg, lane_group, 0)


def _peer_prep_call(x, mod, g_ffn, wqt, k1, k2, *, tile_tokens):
    b, t, d = x.shape
    nb, tt = _token_tile(b, t, tile_tokens)
    rows = nb * tt
    n_tb = t // tt
    ntok = b * t
    n_heads, n_keys, _ = k1.shape
    kern = functools.partial(_peer_prep_kernel, nb=nb, tt=tt)
    blk = lambda i: (i // n_tb, i % n_tb, 0)
    n_lg = rows // LANES
    n_grp = ntok // LANES
    f_shape = jax.ShapeDtypeStruct((n_heads, n_grp, n_keys, LANES), F32)
    b_shape = jax.ShapeDtypeStruct((n_heads, n_grp, n_keys // 2, LANES), F32)
    e_spec = pl.BlockSpec((n_heads, n_lg, n_keys, LANES), lambda i: (0, i, 0, 0))
    b_spec = pl.BlockSpec((n_heads, n_lg, n_keys // 2, LANES), lambda i: (0, i, 0, 0))
    vmem = (2 * rows * d * 4 + 2 * wqt.size * 2 + 2 * rows * d * 2 + 8 * n_heads * n_keys * rows * 4
            + 6 * rows * d * 4 + wqt.shape[0] * rows * 6)
    return pl.pallas_call(
        kern,
        out_shape=(jax.ShapeDtypeStruct((n_grp, d, LANES), BF16), f_shape, f_shape, b_shape, b_shape),
        grid=(ntok // rows,),
        in_specs=[
            pl.BlockSpec((nb, tt, d), blk),
            pl.BlockSpec((nb, N_MOD, d), lambda i: (i // n_tb, 0, 0)),
            pl.BlockSpec((1, d), lambda i: (0, 0)),
            pl.BlockSpec(wqt.shape, lambda i: (0, 0)),
            pl.BlockSpec(k1.shape, lambda i: (0, 0, 0)),
            pl.BlockSpec(k2.shape, lambda i: (0, 0, 0)),
        ],
        out_specs=(pl.BlockSpec((n_lg, d, LANES), lambda i: (i, 0, 0)), e_spec, e_spec, b_spec, b_spec),
        scratch_shapes=[pltpu.VMEM((n_heads, n_lg, n_keys, LANES), F32),
                        pltpu.VMEM((n_heads, n_lg, n_keys, LANES), F32)],
        compiler_params=pltpu.CompilerParams(
            dimension_semantics=("arbitrary",),
            vmem_limit_bytes=_vmem_limit(vmem),
        ),
        name="peer_prep",
    )(x, mod, g_ffn.reshape(1, d), wqt, k1, k2)


def _peer_main_kernel(x_ref, mod_ref, ht_ref, e1n_ref, cnt_ref, e2_ref, r2_ref, u_ref, vt_ref, fg_ref,
                      o_ref, act0_ref, act1_ref, prod0_ref, prod1_ref, acc_ref,
                      *, nb, tt, n_chunks, total, final_norm):
    s = pl.program_id(0)
    n_heads = e2_ref.shape[0]
    n_keys = 2 * e2_ref.shape[2]
    rows = nb * tt
    e_c = u_ref.shape[0]
    d = acc_ref.shape[1]
    s3 = jnp.clip(s - 2, 0, total - 1)
    c3 = s3 % n_chunks

    @pl.when(s == 0)
    def _():
        act1_ref[...] = jnp.zeros(act1_ref.shape, F32)
        prod0_ref[...] = jnp.zeros(prod0_ref.shape, BF16)
        prod1_ref[...] = jnp.zeros(prod1_ref.shape, BF16)

    @pl.when(c3 == 0)
    def _():
        acc_ref[...] = jnp.zeros(acc_ref.shape, F32)

    def step(act_w, act_r, prod_w, prod_r):
        def tile(ii, lg):
            er = slice(ii * n_keys, (ii + 1) * n_keys)
            row_tile = lambda ref, hd: pltpu.bitcast(
                jnp.broadcast_to(ref[hd, lg, ii:ii + 1, :], (n_keys // 2, LANES)), BF16)
            w = None
            for hd in range(n_heads):
                e2t = pltpu.bitcast(e2_ref[hd, lg], BF16)
                r2t = pltpu.bitcast(r2_ref[hd, lg], BF16)
                term = jnp.where(r2t < row_tile(cnt_ref, hd), e2t, 0.0)
                term = term * row_tile(e1n_ref, hd)
                w = term if w is None else w + term
            a = act_r[lg, er, :]
            prod_w[lg, er, :] = _gelu_tanh(a.astype(BF16)) * w

        n_lg = rows // LANES
        assert n_lg % 2 == 0
        tiles = [(ii, lg) for ii in range(e_c // n_keys) for lg in range(n_lg)]
        pieces = []
        for r in range(0, e_c, MXU_PIECE_ROWS):
            pieces += [(1, slice(r, r + MXU_PIECE_ROWS), g) for g in range(0, n_lg, 2)]
        for r in range(0, d, MXU_PIECE_ROWS):
            pieces += [(3, slice(r, r + MXU_PIECE_ROWS), g) for g in range(0, n_lg, 2)]
        per_piece = len(tiles) // len(pieces)
        assert per_piece * len(pieces) == len(tiles)
        for k, (stage, rs, g) in enumerate(pieces):
            if stage == 1:
                rhs = jnp.concatenate([ht_ref[g], ht_ref[g + 1]], axis=1)
                res = jnp.dot(u_ref[rs, :], rhs, preferred_element_type=F32)
                act_w[g, rs, :] = res[:, :LANES]
                act_w[g + 1, rs, :] = res[:, LANES:]
            else:
                rhs = jnp.concatenate([prod_r[g], prod_r[g + 1]], axis=1)
                res = jnp.dot(vt_ref[rs, :], rhs, preferred_element_type=F32)
                acc_ref[g, rs, :] += res[:, :LANES]
                acc_ref[g + 1, rs, :] += res[:, LANES:]
            for ii, lg in tiles[k * per_piece:(k + 1) * per_piece]:
                tile(ii, lg)

    @pl.when(s % 2 == 0)
    def _():
        step(act0_ref, act1_ref, prod1_ref, prod0_ref)

    @pl.when(s % 2 == 1)
    def _():
        step(act1_ref, act0_ref, prod0_ref, prod1_ref)

    @pl.when(jnp.logical_and(c3 == n_chunks - 1, s >= 2))
    def _():
        m = mod_ref[...]
        out = jnp.concatenate([acc_ref[lg].T for lg in range(rows // LANES)], axis=0)
        y = x_ref[...] + m[:, 5:6, :] * out.reshape(nb, tt, -1)
        if final_norm:
            y = (y * lax.rsqrt(jnp.mean(y * y, axis=-1, keepdims=True) + EPS)) * fg_ref[...]
        o_ref[...] = y


def _peer_main_call(x, mod, ht, e1n, cnt, e2, r2, u_bf, vt_bf, final_g, *, tile_tokens, chunk_experts,
                    final_norm):
    b, t, d = x.shape
    nb, tt = _token_tile(b, t, tile_tokens)
    rows = nb * tt
    n_tb = t // tt
    ntok = b * t
    n_blocks = ntok // rows
    n_exp = u_bf.shape[0]
    n_heads, _, n_keys, _ = e1n.shape
    n_lg = rows // LANES
    e_c = chunk_experts
    n_chunks = n_exp // e_c
    n_i = e_c // n_keys
    assert n_i % SUBLANES == 0 and n_chunks > 1
    total = n_blocks * n_chunks
    last = total - 1
    st1 = lambda s: jnp.minimum(s, last)
    st2 = lambda s: jnp.clip(s - 1, 0, last)
    st3 = lambda s: jnp.clip(s - 2, 0, last)
    xblk = lambda s: ((st3(s) // n_chunks) // n_tb, (st3(s) // n_chunks) % n_tb, 0)
    kern = functools.partial(_peer_main_kernel, nb=nb, tt=tt, n_chunks=n_chunks, total=total,
                             final_norm=final_norm)
    vmem = (4 * rows * d * 4 + 2 * d * rows * 2 + 2 * n_heads * n_keys * rows * 4
            + 4 * n_heads * n_i * rows * 4 + 4 * e_c * d * 2 + 2 * e_c * rows * 4
            + 2 * e_c * rows * 2 + d * rows * 4 + 3 * rows * d * 4)
    return pl.pallas_call(
        kern,
        out_shape=jax.ShapeDtypeStruct(x.shape, F32),
        grid=(total + 2,),
        in_specs=[
            pl.BlockSpec((nb, tt, d), xblk),
            pl.BlockSpec((nb, N_MOD, d), lambda s: ((st3(s) // n_chunks) // n_tb, 0, 0)),
            pl.BlockSpec((n_lg, d, LANES), lambda s: (st1(s) // n_chunks, 0, 0)),
            pl.BlockSpec((n_heads, n_lg, n_i, LANES),
                         lambda s: (0, st2(s) // n_chunks, st2(s) % n_chunks, 0)),
            pl.BlockSpec((n_heads, n_lg, n_i, LANES),
                         lambda s: (0, st2(s) // n_chunks, st2(s) % n_chunks, 0)),
            pl.BlockSpec((n_heads, n_lg, n_keys // 2, LANES), lambda s: (0, st2(s) // n_chunks, 0, 0)),
            pl.BlockSpec((n_heads, n_lg, n_keys // 2, LANES), lambda s: (0, st2(s) // n_chunks, 0, 0)),
            pl.BlockSpec((e_c, d), lambda s: (st1(s) % n_chunks, 0)),
            pl.BlockSpec((d, e_c), lambda s: (0, st3(s) % n_chunks)),
            pl.BlockSpec((1, d), lambda s: (0, 0)),
        ],
        out_specs=pl.BlockSpec((nb, tt, d), xblk),
        scratch_shapes=[pltpu.VMEM((n_lg, e_c, LANES), F32), pltpu.VMEM((n_lg, e_c, LANES), F32),
                        pltpu.VMEM((n_lg, e_c, LANES), BF16), pltpu.VMEM((n_lg, e_c, LANES), BF16),
                        pltpu.VMEM((n_lg, d, LANES), F32)],
        compiler_params=pltpu.CompilerParams(
            dimension_semantics=("arbitrary",),
            vmem_limit_bytes=_vmem_limit(vmem),
        ),
        name="peer_main",
    )(x, mod, ht, e1n, cnt, e2, r2, u_bf, vt_bf, final_g.reshape(1, d))


MIX_TILE_TOKENS = 256
PREP_TILE_TOKENS = 256
MAIN_TILE_TOKENS = 512
MAIN_CHUNK_EXPERTS = 1024


def _run_trunk(x, mod, prev_a, prev_b, prev_c, pos0, wts):
    depth = mod.shape[0]
    new_a, new_b, new_c = [], [], []
    for l in range(depth):
        i = l // 2
        if l % 2 == 0:
            x, sa, sb = _mix_even_call(
                x, mod[l], prev_a[i], prev_b[i], wts["norm_mix_g"][l], wts["w_in_even"][i],
                wts["dw_a"][i], wts["db_a"][i], wts["ln_a_g"][i], wts["ln_a_b"][i], wts["w_pool"][i],
                wts["pool_scale"][i], wts["w_out_even"][i], pos0=pos0, tile_tokens=MIX_TILE_TOKENS)
            new_a.append(sa)
            new_b.append(sb)
        else:
            x, sc = _mix_odd_call(x, mod[l], prev_c[i], wts["norm_mix_g"][l], wts["w_in_odd"][i],
                                  wts["dw_c"][i], wts["w_out_odd"][i], tile_tokens=MIX_TILE_TOKENS)
            new_c.append(sc)
        ht, e1n, cnt, e2, r2 = _peer_prep_call(x, mod[l], wts["norm_ffn_g"][l], wts["peer_wqt"][l],
                                           wts["peer_k1"][l], wts["peer_k2"][l],
                                           tile_tokens=PREP_TILE_TOKENS)
        x = _peer_main_call(x, mod[l], ht, e1n, cnt, e2, r2, wts["peer_u"][l], wts["peer_vt"][l],
                            wts["final_g"], tile_tokens=MAIN_TILE_TOKENS,
                            chunk_experts=MAIN_CHUNK_EXPERTS, final_norm=(l == depth - 1))
    return x, jnp.stack(new_a), jnp.stack(new_b), jnp.stack(new_c)


def kernel(x_prompt, x_sample, state_conv_a, state_pool_b, state_conv_c, c_prompt, c_sample, norm_mix_g, norm_ffn_g, w_ada, b_ada, w_in_even, dw_a, db_a, ln_a_g, ln_a_b, w_pool, pool_scale, w_out_even, w_in_odd, dw_c, w_out_odd, peer_wq, peer_k1, peer_k2, peer_u, peer_v, final_g):
    bp, _, d = x_prompt.shape
    bs = x_sample.shape[0]
    depth = w_ada.shape[0]
    wts = dict(
        norm_mix_g=norm_mix_g, norm_ffn_g=norm_ffn_g, final_g=final_g,
        w_in_even=w_in_even.astype(BF16), dw_a=dw_a, db_a=db_a, ln_a_g=ln_a_g, ln_a_b=ln_a_b,
        w_pool=w_pool.astype(BF16), pool_scale=pool_scale, w_out_even=w_out_even.astype(BF16),
        w_in_odd=w_in_odd.astype(BF16), dw_c=dw_c, w_out_odd=w_out_odd.astype(BF16),
        peer_wqt=jnp.swapaxes(peer_wq, 1, 2).astype(BF16),
        peer_k1=peer_k1.astype(BF16), peer_k2=peer_k2.astype(BF16),
        peer_u=peer_u.astype(BF16), peer_vt=jnp.swapaxes(peer_v, 1, 2).astype(BF16),
    )
    mod = _ada_call(jnp.concatenate([c_prompt, c_sample], axis=0), w_ada, b_ada)
    mod = mod.reshape(depth, bp + bs, N_MOD, d)
    n_even, _, ha, d_a = state_conv_a.shape
    _, _, hb, d_b = state_pool_b.shape
    n_odd, _, hc, d_c = state_conv_c.shape
    y_p, pa_a, pa_b, pa_c = _run_trunk(
        x_prompt, mod[:, :bp], jnp.zeros((n_even, bp, ha, d_a), F32),
        jnp.zeros((n_even, bp, hb, d_b), F32), jnp.zeros((n_odd, bp, hc, d_c), F32), 0, wts)
    y_s, sa_a, sa_b, sa_c = _run_trunk(
        x_sample, mod[:, bp:], state_conv_a, state_pool_b, state_conv_c, PAST_LEN, wts)
    return (y_p, y_s, pa_a, pa_b, pa_c, sa_a, sa_b, sa_c)
```

```python
import functools

import jax
import jax.numpy as jnp
from jax import lax
from jax.experimental import pallas as pl
from jax.experimental.pallas import tpu as pltpu

F32 = jnp.float32
BF16 = jnp.bfloat16

PAST_LEN = 4096
POOL_WINDOWS = (2, 4, 8, 16)
PEER_TOPK = 16
N_MOD = 6
EPS = 1e-6

LANES = 128
SUBLANES = 8
VMEM_LIMIT_CAP = 56 * 1024 * 1024

HIST_A = 32
HIST_B = 16
HIST_C = 8
CONV_ROW_CHUNK = 32
MXU_PIECE_ROWS = 512


def _vmem_limit(nbytes):
    return int(min(VMEM_LIMIT_CAP, max(16 * 1024 * 1024, nbytes)))


def _token_tile(batch, seq, target):
    if seq >= target:
        assert seq % target == 0
        return 1, target
    nb = min(batch, target // seq)
    assert batch % nb == 0
    return nb, seq


def _ada_kernel(c_ref, w_ref, b_ref, o_ref):
    c = c_ref[...]
    ca = (c * jax.nn.sigmoid(c)).astype(BF16)
    o_ref[0] = jnp.dot(ca, w_ref[0].astype(BF16), preferred_element_type=F32) + b_ref[0]


def _ada_call(c_all, w_ada, b_ada):
    depth, d, n = w_ada.shape
    bt = c_all.shape[0]
    tn = 1024
    return pl.pallas_call(
        _ada_kernel,
        out_shape=jax.ShapeDtypeStruct((depth, bt, n), F32),
        grid=(depth, n // tn),
        in_specs=[
            pl.BlockSpec((bt, d), lambda l, j: (0, 0)),
            pl.BlockSpec((1, d, tn), lambda l, j: (l, 0, j)),
            pl.BlockSpec((1, 1, tn), lambda l, j: (l, 0, j)),
        ],
        out_specs=pl.BlockSpec((1, bt, tn), lambda l, j: (l, 0, j)),
        compiler_params=pltpu.CompilerParams(
            dimension_semantics=("arbitrary", "arbitrary"),
            vmem_limit_bytes=_vmem_limit(4 * d * tn * 4),
        ),
        name="ada_mod",
    )(c_all, w_ada, b_ada.reshape(depth, 1, n))


def _modulated_rmsnorm(x, g, shift, scale):
    y = x * lax.rsqrt(jnp.mean(x * x, axis=-1, keepdims=True) + EPS)
    return (y * g) * (1.0 + scale) + shift


def _gelu_tanh(x):
    c0 = 0.7978845608028654
    inner = x * (c0 + (c0 * 0.044715) * (x * x))
    return (0.5 * x) * (1.0 + jnp.tanh(inner))


def _mix_even_kernel(x_ref, mod_ref, pa_ref, pb_ref, g_ref, win_ref, dwa_ref, dba_ref, lng_ref,
                     lnb_ref, wpool_ref, pscale_ref, wout_ref,
                     xo_ref, na_ref, nbo_ref, exta_ref, extb_ref, *, nb, tt, pos0, n_tblk):
    t = pl.program_id(1)
    d_a = exta_ref.shape[-1]
    d_b = extb_ref.shape[-1]
    n_tap = dwa_ref.shape[0]
    rows = nb * tt

    @pl.when(t == 0)
    def _():
        exta_ref[:, 0:HIST_A - (n_tap - 1), :] = jnp.zeros((nb, HIST_A - (n_tap - 1), d_a), F32)
        exta_ref[:, HIST_A - (n_tap - 1):HIST_A, :] = pa_ref[...]
        extb_ref[:, 0:1, :] = jnp.zeros((nb, 1, d_b), F32)
        extb_ref[:, 1:HIST_B, :] = pb_ref[...]

    x = x_ref[...]
    m = mod_ref[...]
    h = _modulated_rmsnorm(x, g_ref[...], m[:, 0:1, :], m[:, 1:2, :])
    p = jnp.dot(h.reshape(rows, -1).astype(BF16), win_ref[...], preferred_element_type=F32)
    a = p[:, :d_a] * jax.nn.sigmoid(p[:, d_a:2 * d_a])
    b_in = p[:, 2 * d_a:]
    exta_ref[:, HIST_A:HIST_A + tt, :] = a.reshape(nb, tt, d_a)
    extb_ref[:, HIST_B:HIST_B + tt, :] = b_in.reshape(nb, tt, d_b)

    taps = [dwa_ref[k:k + 1, :] for k in range(n_tap)]
    first = HIST_A - (n_tap - 1)
    pieces = []
    for bi in range(nb):
        for r in range(0, tt, CONV_ROW_CHUNK):
            rc = min(CONV_ROW_CHUNK, tt - r)
            acc = taps[0] * exta_ref[bi, first + r:first + r + rc, :]
            for k in range(1, n_tap):
                acc = acc + taps[k] * exta_ref[bi, first + r + k:first + r + k + rc, :]
            pieces.append(acc)
    conv = jnp.concatenate(pieces, axis=0) + dba_ref[...]
    mu = jnp.mean(conv, axis=-1, keepdims=True)
    cen = conv - mu
    var = jnp.mean(cen * cen, axis=-1, keepdims=True)
    ln = (cen * lax.rsqrt(var + EPS)) * lng_ref[...] + lnb_ref[...]
    a_out = ln * jax.nn.sigmoid(ln)

    dg = d_b // len(POOL_WINDOWS)
    pos = pos0 + t * tt + lax.broadcasted_iota(jnp.int32, (nb, tt, dg), 1)
    b_groups = []
    for gi, w in enumerate(POOL_WINDOWS):
        sl = slice(gi * dg, (gi + 1) * dg)
        s = extb_ref[:, HIST_B:HIST_B + tt, sl]
        for k in range(1, w):
            s = s + extb_ref[:, HIST_B - k:HIST_B - k + tt, sl]
        cnt = jnp.minimum(w, pos + 1).astype(F32)
        pooled = s / cnt - extb_ref[:, HIST_B:HIST_B + tt, sl]
        b_groups.append(jnp.dot(pooled.reshape(rows, dg).astype(BF16), wpool_ref[gi],
                                preferred_element_type=F32))
    b_out = jnp.concatenate(b_groups, axis=-1) * pscale_ref[...]

    ab = jnp.concatenate([a_out, b_out], axis=-1).astype(BF16)
    y = jnp.dot(ab, wout_ref[...], preferred_element_type=F32)
    xo_ref[...] = x + m[:, 2:3, :] * y.reshape(nb, tt, -1)

    @pl.when(t == n_tblk - 1)
    def _():
        na_ref[...] = exta_ref[:, HIST_A + tt - (n_tap - 1):HIST_A + tt, :]
        nbo_ref[...] = extb_ref[:, HIST_B + tt - (HIST_B - 1):HIST_B + tt, :]

    exta_ref[:, 0:HIST_A, :] = exta_ref[:, tt:tt + HIST_A, :]
    extb_ref[:, 0:HIST_B, :] = extb_ref[:, tt:tt + HIST_B, :]


def _mix_even_call(x, mod, prev_a, prev_b, g_mix, w_in, dw_a, db_a, ln_g, ln_b, w_pool, pool_scale,
                   w_out, *, pos0, tile_tokens):
    b, t, d = x.shape
    nb, tt = _token_tile(b, t, tile_tokens)
    n_tblk = t // tt
    d_a = prev_a.shape[-1]
    d_b = prev_b.shape[-1]
    n_tap = dw_a.shape[0]
    assert n_tap - 1 <= HIST_A and prev_b.shape[1] == HIST_B - 1 and tt >= HIST_A
    const2 = lambda i, j: (0, 0)
    kern = functools.partial(_mix_even_kernel, nb=nb, tt=tt, pos0=pos0, n_tblk=n_tblk)
    vmem = (4 * nb * tt * d * 4 + 2 * (w_in.size + w_out.size) * 2
            + nb * (HIST_A + HIST_B + 2 * tt) * d_a * 4 + 12 * nb * tt * d * 4)
    return pl.pallas_call(
        kern,
        out_shape=(jax.ShapeDtypeStruct(x.shape, F32),
                   jax.ShapeDtypeStruct(prev_a.shape, F32),
                   jax.ShapeDtypeStruct(prev_b.shape, F32)),
        grid=(b // nb, n_tblk),
        in_specs=[
            pl.BlockSpec((nb, tt, d), lambda i, j: (i, j, 0)),
            pl.BlockSpec((nb, N_MOD, d), lambda i, j: (i, 0, 0)),
            pl.BlockSpec((nb, n_tap - 1, d_a), lambda i, j: (i, 0, 0)),
            pl.BlockSpec((nb, HIST_B - 1, d_b), lambda i, j: (i, 0, 0)),
            pl.BlockSpec((1, d), const2),
            pl.BlockSpec(w_in.shape, const2),
            pl.BlockSpec(dw_a.shape, const2),
            pl.BlockSpec((1, d_a), const2),
            pl.BlockSpec((1, d_a), const2),
            pl.BlockSpec((1, d_a), const2),
            pl.BlockSpec(w_pool.shape, lambda i, j: (0, 0, 0)),
            pl.BlockSpec((1, d_b), const2),
            pl.BlockSpec(w_out.shape, const2),
        ],
        out_specs=(
            pl.BlockSpec((nb, tt, d), lambda i, j: (i, j, 0)),
            pl.BlockSpec((nb, n_tap - 1, d_a), lambda i, j: (i, 0, 0)),
            pl.BlockSpec((nb, HIST_B - 1, d_b), lambda i, j: (i, 0, 0)),
        ),
        scratch_shapes=[pltpu.VMEM((nb, HIST_A + tt, d_a), F32),
                        pltpu.VMEM((nb, HIST_B + tt, d_b), F32)],
        compiler_params=pltpu.CompilerParams(
            dimension_semantics=("arbitrary", "arbitrary"),
            vmem_limit_bytes=_vmem_limit(vmem),
        ),
        name="mix_even",
    )(x, mod, prev_a, prev_b, g_mix.reshape(1, d), w_in, dw_a, db_a.reshape(1, d_a),
      ln_g.reshape(1, d_a), ln_b.reshape(1, d_a), w_pool, pool_scale.reshape(1, d_b), w_out)


def _mix_odd_kernel(x_ref, mod_ref, pc_ref, g_ref, win_ref, dwc_ref, wout_ref,
                    xo_ref, nc_ref, ext_ref, *, nb, tt, n_tblk):
    t = pl.program_id(1)
    d_c = ext_ref.shape[-1]
    n_tap = dwc_ref.shape[0]
    rows = nb * tt
    first = HIST_C - (n_tap - 1)

    @pl.when(t == 0)
    def _():
        ext_ref[:, 0:first, :] = jnp.zeros((nb, first, d_c), F32)
        ext_ref[:, first:HIST_C, :] = pc_ref[...]

    x = x_ref[...]
    m = mod_ref[...]
    h = _modulated_rmsnorm(x, g_ref[...], m[:, 0:1, :], m[:, 1:2, :])
    p = jnp.dot(h.reshape(rows, -1).astype(BF16), win_ref[...], preferred_element_type=F32)
    bg = p[:, :d_c]
    z = p[:, d_c:2 * d_c] * p[:, 2 * d_c:]
    ext_ref[:, HIST_C:HIST_C + tt, :] = z.reshape(nb, tt, d_c)
    conv = dwc_ref[0:1, :] * ext_ref[:, first:first + tt, :]
    for k in range(1, n_tap):
        conv = conv + dwc_ref[k:k + 1, :] * ext_ref[:, first + k:first + k + tt, :]
    gated = (bg * conv.reshape(rows, d_c)).astype(BF16)
    y = jnp.dot(gated, wout_ref[...], preferred_element_type=F32)
    xo_ref[...] = x + m[:, 2:3, :] * y.reshape(nb, tt, -1)

    @pl.when(t == n_tblk - 1)
    def _():
        nc_ref[...] = ext_ref[:, HIST_C + tt - (n_tap - 1):HIST_C + tt, :]

    ext_ref[:, 0:HIST_C, :] = ext_ref[:, tt:tt + HIST_C, :]


def _mix_odd_call(x, mod, prev_c, g_mix, w_in, dw_c, w_out, *, tile_tokens):
    b, t, d = x.shape
    nb, tt = _token_tile(b, t, tile_tokens)
    n_tblk = t // tt
    d_c = prev_c.shape[-1]
    n_tap = dw_c.shape[0]
    assert n_tap - 1 <= HIST_C and tt >= HIST_C
    const2 = lambda i, j: (0, 0)
    kern = functools.partial(_mix_odd_kernel, nb=nb, tt=tt, n_tblk=n_tblk)
    vmem = (4 * nb * tt * d * 4 + 2 * (w_in.size + w_out.size) * 2
            + nb * (HIST_C + tt) * d_c * 4 + 12 * nb * tt * d * 4)
    return pl.pallas_call(
        kern,
        out_shape=(jax.ShapeDtypeStruct(x.shape, F32), jax.ShapeDtypeStruct(prev_c.shape, F32)),
        grid=(b // nb, n_tblk),
        in_specs=[
            pl.BlockSpec((nb, tt, d), lambda i, j: (i, j, 0)),
            pl.BlockSpec((nb, N_MOD, d), lambda i, j: (i, 0, 0)),
            pl.BlockSpec((nb, n_tap - 1, d_c), lambda i, j: (i, 0, 0)),
            pl.BlockSpec((1, d), const2),
            pl.BlockSpec(w_in.shape, const2),
            pl.BlockSpec(dw_c.shape, const2),
            pl.BlockSpec(w_out.shape, const2),
        ],
        out_specs=(
            pl.BlockSpec((nb, tt, d), lambda i, j: (i, j, 0)),
            pl.BlockSpec((nb, n_tap - 1, d_c), lambda i, j: (i, 0, 0)),
        ),
        scratch_shapes=[pltpu.VMEM((nb, HIST_C + tt, d_c), F32)],
        compiler_params=pltpu.CompilerParams(
            dimension_semantics=("arbitrary", "arbitrary"),
            vmem_limit_bytes=_vmem_limit(vmem),
        ),
        name="mix_odd",
    )(x, mod, prev_c, g_mix.reshape(1, d), w_in, dw_c, w_out)


def _hi(a, b):
    if a is None:
        return b
    if b is None:
        return a
    return jnp.maximum(a, b)


def _lo(a, b):
    if a is None or b is None:
        return None
    return jnp.minimum(a, b)


def _bitonic_merge_desc(xs):
    n = len(xs)
    xs = list(xs)
    j = n // 2
    while j >= 1:
        for i in range(n):
            l = i ^ j
            if l > i:
                xs[i], xs[l] = _hi(xs[i], xs[l]), _lo(xs[i], xs[l])
        j //= 2
    return xs


def _bitonic_sort_desc(xs):
    n = len(xs)
    xs = list(xs)
    k = 2
    while k <= n:
        j = k // 2
        while j >= 1:
            for i in range(n):
                l = i ^ j
                if l > i:
                    hi, lo = _hi(xs[i], xs[l]), _lo(xs[i], xs[l])
                    if (i & k) == 0 or k == n:
                        xs[i], xs[l] = hi, lo
                    else:
                        xs[i], xs[l] = lo, hi
            j //= 2
        k *= 2
    return xs


def _merge_top(a, b):
    n = len(a)
    return _bitonic_merge_desc([_hi(a[k], b[n - 1 - k]) for k in range(n)])


def _top_sorted(xs, k):
    xs = list(xs) + [None] * ((-len(xs)) % k)
    groups = [_bitonic_sort_desc(xs[i:i + k]) for i in range(0, len(xs), k)]
    while len(groups) > 1:
        nxt = [_merge_top(groups[i], groups[i + 1]) for i in range(0, len(groups) - 1, 2)]
        if len(groups) % 2:
            nxt.append(groups[-1])
        groups = nxt
    return groups[0]


def _top_over_sublanes(vregs, k):
    xs = _bitonic_sort_desc(vregs)
    shift = SUBLANES // 2
    while shift >= 1:
        rolled = [pltpu.roll(x, shift, 0) for x in xs]
        xs = _merge_top(xs, rolled)
        shift //= 2
    return xs


def _dup_bf16_words(x):
    u = pltpu.bitcast(x, jnp.int32)
    u = u + (0x7FFF + (lax.shift_right_logical(u, 16) & 1))
    hi = u & jnp.int32(-65536)
    return pltpu.bitcast(hi | lax.shift_right_logical(hi, 16), F32)


def _dup_count_constant(c):
    import numpy as np
    hi = np.asarray(float(c), np.float32).view(np.uint32) >> 16
    return float(((hi << 16) | hi).astype(np.uint32).view(np.float32))


_DUP_COUNT = [_dup_count_constant(c) for c in range(PEER_TOPK + 1)]


def _peer_group_factors(load_s1, load_s2, n_heads, n_vreg):
    sub = lax.broadcasted_iota(jnp.int32, (SUBLANES, LANES), 0)
    pairs = [(a, b) for a in range(PEER_TOPK) for b in range(PEER_TOPK)
             if (a + 1) * (b + 1) <= PEER_TOPK]
    v1 = [jnp.zeros((SUBLANES, LANES), F32)] * PEER_TOPK
    v2 = [jnp.zeros((SUBLANES, LANES), F32)] * PEER_TOPK
    for hd in range(n_heads):
        t1 = _top_over_sublanes([load_s1(hd, v) for v in range(n_vreg)], PEER_TOPK)
        t2 = _top_over_sublanes([load_s2(hd, v) for v in range(n_vreg)], PEER_TOPK)
        v1 = [jnp.where(sub == hd, t1[a], v1[a]) for a in range(PEER_TOPK)]
        v2 = [jnp.where(sub == hd, t2[a], v2[a]) for a in range(PEER_TOPK)]
    top = _top_sorted([v1[a] + v2[b] for a, b in pairs], PEER_TOPK)
    tau = top[PEER_TOPK - 1]
    z = jnp.ones((SUBLANES, LANES), F32)
    for k in range(1, PEER_TOPK):
        z = z + jnp.exp(top[k] - top[0])
    inv_z = 1.0 / z
    for hd in range(n_heads):
        row = lambda arr: jnp.broadcast_to(arr[hd:hd + 1, :], (SUBLANES, LANES))
        m1, m2, iz, tau_h = row(v1[0]), row(v2[0]), row(inv_z), row(tau)
        t2 = [row(v2[b]) for b in range(PEER_TOPK)]
        for vp in range(n_vreg // 2):
            e1s, cnts, e2s, r2s = [], [], [], []
            for v in (2 * vp, 2 * vp + 1):
                s1v = load_s1(hd, v)
                s2v = load_s2(hd, v)
                cnt = jnp.zeros((SUBLANES, LANES), F32)
                for b in range(PEER_TOPK):
                    cnt = jnp.where(s1v + t2[b] >= tau_h, _DUP_COUNT[b + 1], cnt)
                r2 = jnp.full((SUBLANES, LANES), float(PEER_TOPK), F32)
                for b in reversed(range(PEER_TOPK)):
                    r2 = jnp.where(s2v >= t2[b], float(b), r2)
                e1s.append(_dup_bf16_words(jnp.exp(s1v - m1) * iz))
                cnts.append(cnt)
                e2s.append(jnp.exp(s2v - m2))
                r2s.append(r2)
            yield (hd, vp, jnp.concatenate(e1s, axis=0), jnp.concatenate(cnts, axis=0),
                   jnp.concatenate(e2s, axis=0), jnp.concatenate(r2s, axis=0))


def _peer_prep_kernel(x_ref, mod_ref, g_ref, wqt_ref, k1_ref, k2_ref,
                      ht_ref, e1n_ref, cnt_ref, e2_ref, r2_ref, s1_ref, s2_ref, *, nb, tt):
    rows = nb * tt
    n_heads, n_keys, d_half = k1_ref.shape
    x = x_ref[...]
    m = mod_ref[...]
    h = _modulated_rmsnorm(x, g_ref[...], m[:, 3:4, :], m[:, 4:5, :]).reshape(rows, -1)
    ht = h.T.astype(BF16)
    n_lg = rows // LANES
    for lg in range(n_lg):
        ht_ref[lg] = ht[:, lg * LANES:(lg + 1) * LANES]
    qt = jnp.dot(wqt_ref[...], ht, preferred_element_type=F32).astype(BF16)
    for hd in range(n_heads):
        base = hd * 2 * d_half
        sc1 = jnp.dot(k1_ref[hd], qt[base:base + d_half], preferred_element_type=F32)
        sc2 = jnp.dot(k2_ref[hd], qt[base + d_half:base + 2 * d_half], preferred_element_type=F32)
        for lg in range(n_lg):
            s1_ref[hd, lg] = sc1[:, lg * LANES:(lg + 1) * LANES]
            s2_ref[hd, lg] = sc2[:, lg * LANES:(lg + 1) * LANES]

    n_vreg = n_keys // SUBLANES
    assert n_vreg == PEER_TOPK and n_heads == SUBLANES

    def lane_group(lg, carry):
        load_s1 = lambda hd, v: s1_ref[hd, lg, v * SUBLANES:(v + 1) * SUBLANES, :]
        load_s2 = lambda hd, v: s2_ref[hd, lg, v * SUBLANES:(v + 1) * SUBLANES, :]
        for hd, vp, e1d, cntd, e2, r2 in _peer_group_factors(load_s1, load_s2, n_heads, n_vreg):
            rs = slice(vp * 2 * SUBLANES, (vp + 1) * 2 * SUBLANES)
            e1n_ref[hd, lg, rs, :] = e1d
            cnt_ref[hd, lg, rs, :] = cntd
            ws = slice(vp * SUBLANES, (vp + 1) * SUBLANES)
            e2_ref[hd, lg, ws, :] = pltpu.bitcast(e2.astype(BF16), F32)
            r2_ref[hd, lg, ws, :] = pltpu.bitcast(r2.astype(BF16), F32)
        return carry

    lax.fori_loop(0, n_lg, lane_group, 0)


def _peer_prep_call(x, mod, g_ffn, wqt, k1, k2, *, tile_tokens):
    b, t, d = x.shape
    nb, tt = _token_tile(b, t, tile_tokens)
    rows = nb * tt
    n_tb = t // tt
    ntok = b * t
    n_heads, n_keys, _ = k1.shape
    kern = functools.partial(_peer_prep_kernel, nb=nb, tt=tt)
    blk = lambda i: (i // n_tb, i % n_tb, 0)
    n_lg = rows // LANES
    n_grp = ntok // LANES
    f_shape = jax.ShapeDtypeStruct((n_heads, n_grp, n_keys, LANES), F32)
    b_shape = jax.ShapeDtypeStruct((n_heads, n_grp, n_keys // 2, LANES), F32)
    e_spec = pl.BlockSpec((n_heads, n_lg, n_keys, LANES), lambda i: (0, i, 0, 0))
    b_spec = pl.BlockSpec((n_heads, n_lg, n_keys // 2, LANES), lambda i: (0, i, 0, 0))
    vmem = (2 * rows * d * 4 + 2 * wqt.size * 2 + 2 * rows * d * 2 + 8 * n_heads * n_keys * rows * 4
            + 6 * rows * d * 4 + wqt.shape[0] * rows * 6)
    return pl.pallas_call(
        kern,
        out_shape=(jax.ShapeDtypeStruct((n_grp, d, LANES), BF16), f_shape, f_shape, b_shape, b_shape),
        grid=(ntok // rows,),
        in_specs=[
            pl.BlockSpec((nb, tt, d), blk),
            pl.BlockSpec((nb, N_MOD, d), lambda i: (i // n_tb, 0, 0)),
            pl.BlockSpec((1, d), lambda i: (0, 0)),
            pl.BlockSpec(wqt.shape, lambda i: (0, 0)),
            pl.BlockSpec(k1.shape, lambda i: (0, 0, 0)),
            pl.BlockSpec(k2.shape, lambda i: (0, 0, 0)),
        ],
        out_specs=(pl.BlockSpec((n_lg, d, LANES), lambda i: (i, 0, 0)), e_spec, e_spec, b_spec, b_spec),
        scratch_shapes=[pltpu.VMEM((n_heads, n_lg, n_keys, LANES), F32),
                        pltpu.VMEM((n_heads, n_lg, n_keys, LANES), F32)],
        compiler_params=pltpu.CompilerParams(
            dimension_semantics=("arbitrary",),
            vmem_limit_bytes=_vmem_limit(vmem),
        ),
        name="peer_prep",
    )(x, mod, g_ffn.reshape(1, d), wqt, k1, k2)


def _peer_main_kernel(x_ref, mod_ref, ht_ref, e1n_ref, cnt_ref, e2_ref, r2_ref, u_ref, vt_ref, fg_ref,
                      o_ref, act0_ref, act1_ref, prod0_ref, prod1_ref, acc_ref,
                      *, nb, tt, n_chunks, total, final_norm):
    s = pl.program_id(0)
    n_heads = e2_ref.shape[0]
    n_keys = 2 * e2_ref.shape[2]
    rows = nb * tt
    e_c = u_ref.shape[0]
    d = acc_ref.shape[1]
    s3 = jnp.clip(s - 2, 0, total - 1)
    c3 = s3 % n_chunks

    @pl.when(s == 0)
    def _():
        act1_ref[...] = jnp.zeros(act1_ref.shape, F32)
        prod0_ref[...] = jnp.zeros(prod0_ref.shape, BF16)
        prod1_ref[...] = jnp.zeros(prod1_ref.shape, BF16)

    @pl.when(c3 == 0)
    def _():
        acc_ref[...] = jnp.zeros(acc_ref.shape, F32)

    def step(act_w, act_r, prod_w, prod_r):
        def tile(ii, lg):
            er = slice(ii * n_keys, (ii + 1) * n_keys)
            row_tile = lambda ref, hd: pltpu.bitcast(
                jnp.broadcast_to(ref[hd, lg, ii:ii + 1, :], (n_keys // 2, LANES)), BF16)
            w = None
            for hd in range(n_heads):
                e2t = pltpu.bitcast(e2_ref[hd, lg], BF16)
                r2t = pltpu.bitcast(r2_ref[hd, lg], BF16)
                term = jnp.where(r2t < row_tile(cnt_ref, hd), e2t, 0.0)
                term = term * row_tile(e1n_ref, hd)
                w = term if w is None else w + term
            a = act_r[lg, er, :]
            prod_w[lg, er, :] = _gelu_tanh(a.astype(BF16)) * w

        n_lg = rows // LANES
        assert n_lg % 2 == 0
        tiles = [(ii, lg) for ii in range(e_c // n_keys) for lg in range(n_lg)]
        pieces = []
        for r in range(0, e_c, MXU_PIECE_ROWS):
            pieces += [(1, slice(r, r + MXU_PIECE_ROWS), g) for g in range(0, n_lg, 2)]
        for r in range(0, d, MXU_PIECE_ROWS):
            pieces += [(3, slice(r, r + MXU_PIECE_ROWS), g) for g in range(0, n_lg, 2)]
        per_piece = len(tiles) // len(pieces)
        assert per_piece * len(pieces) == len(tiles)
        for k, (stage, rs, g) in enumerate(pieces):
            if stage == 1:
                rhs = jnp.concatenate([ht_ref[g], ht_ref[g + 1]], axis=1)
                res = jnp.dot(u_ref[rs, :], rhs, preferred_element_type=F32)
                act_w[g, rs, :] = res[:, :LANES]
                act_w[g + 1, rs, :] = res[:, LANES:]
            else:
                rhs = jnp.concatenate([prod_r[g], prod_r[g + 1]], axis=1)
                res = jnp.dot(vt_ref[rs, :], rhs, preferred_element_type=F32)
                acc_ref[g, rs, :] += res[:, :LANES]
                acc_ref[g + 1, rs, :] += res[:, LANES:]
            for ii, lg in tiles[k * per_piece:(k + 1) * per_piece]:
                tile(ii, lg)

    @pl.when(s % 2 == 0)
    def _():
        step(act0_ref, act1_ref, prod1_ref, prod0_ref)

    @pl.when(s % 2 == 1)
    def _():
        step(act1_ref, act0_ref, prod0_ref, prod1_ref)

    @pl.when(jnp.logical_and(c3 == n_chunks - 1, s >= 2))
    def _():
        m = mod_ref[...]
        out = jnp.concatenate([acc_ref[lg].T for lg in range(rows // LANES)], axis=0)
        y = x_ref[...] + m[:, 5:6, :] * out.reshape(nb, tt, -1)
        if final_norm:
            y = (y * lax.rsqrt(jnp.mean(y * y, axis=-1, keepdims=True) + EPS)) * fg_ref[...]
        o_ref[...] = y


def _peer_main_call(x, mod, ht, e1n, cnt, e2, r2, u_bf, vt_bf, final_g, *, tile_tokens, chunk_experts,
                    final_norm):
    b, t, d = x.shape
    nb, tt = _token_tile(b, t, tile_tokens)
    rows = nb * tt
    n_tb = t // tt
    ntok = b * t
    n_blocks = ntok // rows
    n_exp = u_bf.shape[0]
    n_heads, _, n_keys, _ = e1n.shape
    n_lg = rows // LANES
    e_c = chunk_experts
    n_chunks = n_exp // e_c
    n_i = e_c // n_keys
    assert n_i % SUBLANES == 0 and n_chunks > 1
    total = n_blocks * n_chunks
    last = total - 1
    st1 = lambda s: jnp.minimum(s, last)
    st2 = lambda s: jnp.clip(s - 1, 0, last)
    st3 = lambda s: jnp.clip(s - 2, 0, last)
    xblk = lambda s: ((st3(s) // n_chunks) // n_tb, (st3(s) // n_chunks) % n_tb, 0)
    kern = functools.partial(_peer_main_kernel, nb=nb, tt=tt, n_chunks=n_chunks, total=total,
                             final_norm=final_norm)
    vmem = (4 * rows * d * 4 + 2 * d * rows * 2 + 2 * n_heads * n_keys * rows * 4
            + 4 * n_heads * n_i * rows * 4 + 4 * e_c * d * 2 + 2 * e_c * rows * 4
            + 2 * e_c * rows * 2 + d * rows * 4 + 3 * rows * d * 4)
    return pl.pallas_call(
        kern,
        out_shape=jax.ShapeDtypeStruct(x.shape, F32),
        grid=(total + 2,),
        in_specs=[
            pl.BlockSpec((nb, tt, d), xblk),
            pl.BlockSpec((nb, N_MOD, d), lambda s: ((st3(s) // n_chunks) // n_tb, 0, 0)),
            pl.BlockSpec((n_lg, d, LANES), lambda s: (st1(s) // n_chunks, 0, 0)),
            pl.BlockSpec((n_heads, n_lg, n_i, LANES),
                         lambda s: (0, st2(s) // n_chunks, st2(s) % n_chunks, 0)),
            pl.BlockSpec((n_heads, n_lg, n_i, LANES),
                         lambda s: (0, st2(s) // n_chunks, st2(s) % n_chunks, 0)),
            pl.BlockSpec((n_heads, n_lg, n_keys // 2, LANES), lambda s: (0, st2(s) // n_chunks, 0, 0)),
            pl.BlockSpec((n_heads, n_lg, n_keys // 2, LANES), lambda s: (0, st2(s) // n_chunks, 0, 0)),
            pl.BlockSpec((e_c, d), lambda s: (st1(s) % n_chunks, 0)),
            pl.BlockSpec((d, e_c), lambda s: (0, st3(s) % n_chunks)),
            pl.BlockSpec((1, d), lambda s: (0, 0)),
        ],
        out_specs=pl.BlockSpec((nb, tt, d), xblk),
        scratch_shapes=[pltpu.VMEM((n_lg, e_c, LANES), F32), pltpu.VMEM((n_lg, e_c, LANES), F32),
                        pltpu.VMEM((n_lg, e_c, LANES), BF16), pltpu.VMEM((n_lg, e_c, LANES), BF16),
                        pltpu.VMEM((n_lg, d, LANES), F32)],
        compiler_params=pltpu.CompilerParams(
            dimension_semantics=("arbitrary",),
            vmem_limit_bytes=_vmem_limit(vmem),
        ),
        name="peer_main",
    )(x, mod, ht, e1n, cnt, e2, r2, u_bf, vt_bf, final_g.reshape(1, d))


MIX_TILE_TOKENS = 256
PREP_TILE_TOKENS = 256
MAIN_TILE_TOKENS = 512
MAIN_CHUNK_EXPERTS = 1024


def _run_trunk(x, mod, prev_a, prev_b, prev_c, pos0, wts):
    depth = mod.shape[0]
    new_a, new_b, new_c = [], [], []
    for l in range(depth):
        i = l // 2
        if l % 2 == 0:
            x, sa, sb = _mix_even_call(
                x, mod[l], prev_a[i], prev_b[i], wts["norm_mix_g"][l], wts["w_in_even"][i],
                wts["dw_a"][i], wts["db_a"][i], wts["ln_a_g"][i], wts["ln_a_b"][i], wts["w_pool"][i],
                wts["pool_scale"][i], wts["w_out_even"][i], pos0=pos0, tile_tokens=MIX_TILE_TOKENS)
            new_a.append(sa)
            new_b.append(sb)
        else:
            x, sc = _mix_odd_call(x, mod[l], prev_c[i], wts["norm_mix_g"][l], wts["w_in_odd"][i],
                                  wts["dw_c"][i], wts["w_out_odd"][i], tile_tokens=MIX_TILE_TOKENS)
            new_c.append(sc)
        ht, e1n, cnt, e2, r2 = _peer_prep_call(x, mod[l], wts["norm_ffn_g"][l], wts["peer_wqt"][l],
                                           wts["peer_k1"][l], wts["peer_k2"][l],
                                           tile_tokens=PREP_TILE_TOKENS)
        x = _peer_main_call(x, mod[l], ht, e1n, cnt, e2, r2, wts["peer_u"][l], wts["peer_vt"][l],
                            wts["final_g"], tile_tokens=MAIN_TILE_TOKENS,
                            chunk_experts=MAIN_CHUNK_EXPERTS, final_norm=(l == depth - 1))
    return x, jnp.stack(new_a), jnp.stack(new_b), jnp.stack(new_c)


def kernel(x_prompt, x_sample, state_conv_a, state_pool_b, state_conv_c, c_prompt, c_sample, norm_mix_g, norm_ffn_g, w_ada, b_ada, w_in_even, dw_a, db_a, ln_a_g, ln_a_b, w_pool, pool_scale, w_out_even, w_in_odd, dw_c, w_out_odd, peer_wq, peer_k1, peer_k2, peer_u, peer_v, final_g):
    bp, _, d = x_prompt.shape
    bs = x_sample.shape[0]
    depth = w_ada.shape[0]
    wts = dict(
        norm_mix_g=norm_mix_g, norm_ffn_g=norm_ffn_g, final_g=final_g,
        w_in_even=w_in_even.astype(BF16), dw_a=dw_a, db_a=db_a, ln_a_g=ln_a_g, ln_a_b=ln_a_b,
        w_pool=w_pool.astype(BF16), pool_scale=pool_scale, w_out_even=w_out_even.astype(BF16),
        w_in_odd=w_in_odd.astype(BF16), dw_c=dw_c, w_out_odd=w_out_odd.astype(BF16),
        peer_wqt=jnp.swapaxes(peer_wq, 1, 2).astype(BF16),
        peer_k1=peer_k1.astype(BF16), peer_k2=peer_k2.astype(BF16),
        peer_u=peer_u.astype(BF16), peer_vt=jnp.swapaxes(peer_v, 1, 2).astype(BF16),
    )
    mod = _ada_call(jnp.concatenate([c_prompt, c_sample], axis=0), w_ada, b_ada)
    mod = mod.reshape(depth, bp + bs, N_MOD, d)
    n_even, _, ha, d_a = state_conv_a.shape
    _, _, hb, d_b = state_pool_b.shape
    n_odd, _, hc, d_c = state_conv_c.shape
    y_p, pa_a, pa_b, pa_c = _run_trunk(
        x_prompt, mod[:, :bp], jnp.zeros((n_even, bp, ha, d_a), F32),
        jnp.zeros((n_even, bp, hb, d_b), F32), jnp.zeros((n_odd, bp, hc, d_c), F32), 0, wts)
    y_s, sa_a, sa_b, sa_c = _run_trunk(
        x_sample, mod[:, bp:], state_conv_a, state_pool_b, state_conv_c, PAST_LEN, wts)
    return (y_p, y_s, pa_a, pa_b, pa_c, sa_a, sa_b, sa_c)
```
